```python
import math
import jax, jax.numpy as jnp
from jax import lax
import numpy as np

D_MODEL = 1024
BATCH = 2
SEQ = 16384
DEPTH = 2

N_MEM = 256
D_MIX = 2 * D_MODEL
GROUP = D_MIX // 4
A_HEADS = 4
A_DK = GROUP // A_HEADS
A_DV = GROUP // A_HEADS
B_HEADS = 4
B_DQK = GROUP // B_HEADS // 2
B_DV = GROUP // B_HEADS
C_HEADS = 4
C_D = GROUP // C_HEADS
X_HEADS = 4
X_D = GROUP // X_HEADS

ROPE_THETA = 500000.0
ROPE_DIM = B_DQK // 4
CHUNK = 64
Q_BLOCK = 128
CONV_K = 4
EPS = 1e-6
NEG = -1e30
TINY = 1e-30

IN_SIZES = (GROUP, GROUP, GROUP, GROUP,
            GROUP, GROUP, GROUP, GROUP,
            GROUP, GROUP, GROUP, GROUP, C_HEADS, C_HEADS, GROUP,
            GROUP, GROUP)
N_IN = sum(IN_SIZES)

kernel_name = 'hymba_hgrn2_diffattn_mlstm_memxattn'


def rms_norm(x, g):
    xf = x.astype(jnp.float32)
    y = xf * lax.rsqrt(jnp.mean(xf * xf, axis=-1, keepdims=True) + EPS)
    return (y * g.astype(jnp.float32)).astype(x.dtype)


def split_cols(p, sizes):
    outs, off = [], 0
    for s in sizes:
        outs.append(p[..., off:off + s])
        off += s
    return outs


def partial_rope(x, cos, sin):
    half = ROPE_DIM // 2
    x1 = x[..., :half].astype(jnp.float32)
    x2 = x[..., half:ROPE_DIM].astype(jnp.float32)
    rot = jnp.concatenate([x1 * cos - x2 * sin, x2 * cos + x1 * sin], axis=-1).astype(x.dtype)
    return jnp.concatenate([rot, x[..., ROPE_DIM:]], axis=-1)


def causal_dwconv(x, w, b):
    y = lax.conv_general_dilated(x, w[:, None, :].astype(x.dtype), window_strides=(1,),
                                 padding=((CONV_K - 1, 0),),
                                 dimension_numbers=('NWC', 'WIO', 'NWC'),
                                 feature_group_count=x.shape[-1])
    return y + b.astype(x.dtype)


def to_chunks(t):
    bsz, s = t.shape[:2]
    t = t.reshape((bsz, s // CHUNK, CHUNK) + t.shape[2:])
    return jnp.transpose(t, (1, 0, 3, 2) + tuple(range(4, t.ndim)))


def from_chunks(o):
    nc, bsz, h, c, d = o.shape
    return jnp.transpose(o, (1, 0, 3, 2, 4)).reshape(bsz, nc * c, h, d)


def hgrn2_scan(q, k, v, log_f):
    out_dtype = v.dtype
    qc, kc, vc, gc = [to_chunks(t.astype(jnp.float32)) for t in (q, k, v, log_f)]
    bsz, h, dk, dv = q.shape[0], q.shape[2], q.shape[3], v.shape[3]
    causal = jnp.tril(jnp.ones((CHUNK, CHUNK), dtype=bool))[:, :, None]

    def step(state, inp):
        qi, ki, vi, gi = inp
        b = jnp.cumsum(gi, axis=2)
        rel = b[:, :, :, None, :] - b[:, :, None, :, :]
        decay = jnp.where(causal, jnp.exp(jnp.where(causal, rel, 0.0)), 0.0)
        attn = jnp.einsum('bhtk,bhsk,bhtsk->bhts', qi, ki, decay)
        o = (jnp.einsum('bhts,bhsv->bhtv', attn, vi)
             + jnp.einsum('bhtk,bhkv->bhtv', qi * jnp.exp(b), state))
        b_last = b[:, :, -1]
        state = (jnp.exp(b_last)[..., None] * state
                 + jnp.einsum('bhsk,bhsv->bhkv', ki * jnp.exp(b_last[:, :, None] - b), vi))
        return state, o

    s0 = jnp.zeros((bsz, h, dk, dv), jnp.float32)
    _, o = lax.scan(step, s0, (qc, kc, vc, gc))
    return from_chunks(o).astype(out_dtype)


def mlstm_scan(q, k, v, log_i, log_f):
    out_dtype = v.dtype
    qc, kc, vc = [to_chunks(t.astype(jnp.float32)) for t in (q, k, v)]
    ic, fc = [to_chunks(t.astype(jnp.float32)) for t in (log_i, log_f)]
    bsz, h, d = q.shape[0], q.shape[2], q.shape[3]
    causal = jnp.tril(jnp.ones((CHUNK, CHUNK), dtype=bool))

    def step(carry, inp):
        cmat, n, m = carry
        qi, ki, vi, li, lf = inp
        b = jnp.cumsum(lf, axis=-1)
        dmat = jnp.where(causal, b[..., :, None] - b[..., None, :] + li[..., None, :], NEG)
        inter = b + m[..., None]
        m_t = jnp.maximum(jnp.max(dmat, axis=-1), inter)
        w = jnp.where(causal, jnp.exp(dmat - m_t[..., None]), 0.0)
        w_inter = jnp.exp(inter - m_t)
        s = jnp.einsum('bhtd,bhsd->bhts', qi, ki) * w
        num = (jnp.einsum('bhts,bhsv->bhtv', s, vi)
               + w_inter[..., None] * jnp.einsum('bhtk,bhvk->bhtv', qi, cmat))
        nq = jnp.sum(s, axis=-1) + w_inter * jnp.einsum('bhtk,bhk->bht', qi, n)
        hout = num / jnp.maximum(jnp.abs(nq), jnp.exp(-m_t))[..., None]
        b_last = b[..., -1]
        g = b_last[..., None] - b + li
        m_new = jnp.maximum(b_last + m, jnp.max(g, axis=-1))
        wg = jnp.exp(g - m_new[..., None])
        dec = jnp.exp(b_last + m - m_new)
        cmat = dec[..., None, None] * cmat + jnp.einsum('bhs,bhsv,bhsk->bhvk', wg, vi, ki)
        n = dec[..., None] * n + jnp.einsum('bhs,bhsk->bhk', wg, ki)
        return (cmat, n, m_new), hout

    init = (jnp.zeros((bsz, h, d, d), jnp.float32), jnp.zeros((bsz, h, d), jnp.float32),
            jnp.zeros((bsz, h), jnp.float32))
    _, o = lax.scan(step, init, (qc, kc, vc, ic, fc))
    return from_chunks(o).astype(out_dtype)


def diff_attention(q, k, v, lam):
    bsz, s, h = q.shape[:3]
    nb = s // Q_BLOCK
    qb = q.reshape(bsz, nb, Q_BLOCK, h, 2, q.shape[-1]).transpose(1, 0, 2, 3, 4, 5)
    kpos = jnp.arange(s)

    def block(args):
        qi, j = args
        sc = jnp.einsum('bqhcd,bkhcd->bhcqk', qi, k).astype(jnp.float32)
        qpos = j * Q_BLOCK + jnp.arange(Q_BLOCK)
        sc = jnp.where(qpos[:, None] >= kpos[None, :], sc, NEG)
        p = jax.nn.softmax(sc, axis=-1)
        a = (p[:, :, 0] - lam * p[:, :, 1]).astype(v.dtype)
        return jnp.einsum('bhqk,bkhd->bqhd', a, v)

    o = lax.map(block, (qb, jnp.arange(nb)))
    return o.transpose(1, 0, 2, 3, 4).reshape(bsz, s, h, v.shape[-1])


def hybrid_layer(x, mem, cos, sin, layer_idx, lb, norm_g, w_in, mlstm_gate_b, hgrn_norm_g,
                 diff_qk_norm_g, diff_lambda, diff_subln_g, mlstm_conv_w, mlstm_conv_b,
                 mlstm_norm_g, mem_norm_g, w_mem_kv, xattn_qk_norm_g, w_out):
    bsz, s, _ = x.shape
    h = rms_norm(x, norm_g)
    proj = h @ w_in.astype(h.dtype)
    (a_q, a_f, a_i, a_g, b_q, b_k, b_v, b_g,
     c_q, c_k, c_v, c_o, c_i, c_f, c_g, x_q, x_g) = split_cols(proj, IN_SIZES)

    qa = jax.nn.silu(a_q).reshape(bsz, s, A_HEADS, A_DK) * (A_DK ** -0.5)
    fa = a_f.astype(jnp.float32).reshape(bsz, s, A_HEADS, A_DK)
    lbh = lb.astype(jnp.float32).reshape(A_HEADS, A_DK)
    f_gate = lbh + (1.0 - lbh) * jax.nn.sigmoid(fa)
    log_f = jnp.log(jnp.maximum(f_gate, TINY))
    ka = (1.0 - lbh) * jax.nn.sigmoid(-fa)
    oa = hgrn2_scan(qa, ka, a_i.reshape(bsz, s, A_HEADS, A_DV), log_f)
    ya = rms_norm(oa, hgrn_norm_g).reshape(bsz, s, GROUP) * jax.nn.silu(a_g)

    qb = rms_norm(b_q.reshape(bsz, s, B_HEADS, 2, B_DQK), diff_qk_norm_g[0])
    kb = rms_norm(b_k.reshape(bsz, s, B_HEADS, 2, B_DQK), diff_qk_norm_g[1])
    qb = partial_rope(qb, cos, sin) * (B_DQK ** -0.5)
    kb = partial_rope(kb, cos, sin)
    lam_init = 0.8 - 0.6 * math.exp(-0.3 * layer_idx)
    lp = diff_lambda.astype(jnp.float32)
    lam = jnp.exp(jnp.sum(lp[0] * lp[1])) - jnp.exp(jnp.sum(lp[2] * lp[3])) + lam_init
    ob = diff_attention(qb, kb, b_v.reshape(bsz, s, B_HEADS, B_DV), lam)
    yb = (rms_norm(ob, diff_subln_g) * (1.0 - lam_init)).reshape(bsz, s, GROUP) * jax.nn.silu(b_g)

    qk = jax.nn.silu(causal_dwconv(jnp.concatenate([c_q, c_k], axis=-1), mlstm_conv_w, mlstm_conv_b))
    qc = qk[..., :GROUP].reshape(bsz, s, C_HEADS, C_D)
    kc = qk[..., GROUP:].reshape(bsz, s, C_HEADS, C_D) * (C_D ** -0.5)
    vc = c_v.reshape(bsz, s, C_HEADS, C_D)
    gb = mlstm_gate_b.astype(jnp.float32)
    log_i = c_i.astype(jnp.float32) + gb[:C_HEADS]
    log_fc = jax.nn.log_sigmoid(c_f.astype(jnp.float32) + gb[C_HEADS:])
    hc = mlstm_scan(qc, kc, vc, log_i, log_fc)
    hc = jax.nn.sigmoid(c_o).reshape(bsz, s, C_HEADS, C_D) * hc
    yc = rms_norm(hc, mlstm_norm_g).reshape(bsz, s, GROUP) * jax.nn.silu(c_g)

    mn = rms_norm(mem, mem_norm_g)
    kv = mn @ w_mem_kv.astype(mn.dtype)
    km = rms_norm(kv[..., :GROUP].reshape(bsz, N_MEM, X_HEADS, X_D), xattn_qk_norm_g[1])
    vm = kv[..., GROUP:].reshape(bsz, N_MEM, X_HEADS, X_D)
    qx = rms_norm(x_q.reshape(bsz, s, X_HEADS, X_D), xattn_qk_norm_g[0]) * (X_D ** -0.5)
    px = jax.nn.softmax(jnp.einsum('bqhd,bmhd->bhqm', qx, km).astype(jnp.float32), axis=-1)
    ox = jnp.einsum('bhqm,bmhd->bqhd', px.astype(vm.dtype), vm).reshape(bsz, s, GROUP)
    yx = ox * jax.nn.silu(x_g)

    y = jnp.concatenate([ya, yb, yc, yx], axis=-1).astype(x.dtype) @ w_out.astype(x.dtype)
    return x + y


def setup_inputs(seed: int = 0) -> dict:
    key = jax.random.key(seed)
    ks = jax.random.split(key, 20)
    f32 = jnp.float32
    nrm = lambda k, shape: jax.random.normal(k, shape, f32)
    x = nrm(ks[0], (BATCH, SEQ, D_MODEL))
    mem = nrm(ks[1], (BATCH, N_MEM, D_MODEL))
    offs = jax.random.randint(ks[2], (BATCH, 1), 0, 4096, dtype=jnp.int32)
    positions = offs + jnp.arange(SEQ, dtype=jnp.int32)[None, :]
    norm_g = 1.0 + 0.02 * nrm(ks[3], (DEPTH, D_MODEL))
    w_in = nrm(ks[4], (DEPTH, D_MODEL, N_IN)) * (D_MODEL ** -0.5)
    mlstm_gate_b = jnp.concatenate([
        0.1 * nrm(ks[5], (DEPTH, C_HEADS)),
        jnp.linspace(3.0, 6.0, C_HEADS, dtype=f32)[None, :] + 0.1 * nrm(ks[6], (DEPTH, C_HEADS))], axis=-1)
    hgrn_lb_logits = nrm(ks[7], (DEPTH, GROUP))
    hgrn_norm_g = 1.0 + 0.02 * nrm(ks[8], (DEPTH, A_DV))
    diff_qk_norm_g = 1.0 + 0.02 * nrm(ks[9], (DEPTH, 2, B_DQK))
    diff_lambda = 0.1 * nrm(ks[10], (DEPTH, 4, B_DQK))
    diff_subln_g = 1.0 + 0.02 * nrm(ks[11], (DEPTH, B_DV))
    mlstm_conv_w = nrm(ks[12], (DEPTH, CONV_K, 2 * GROUP)) * (CONV_K ** -0.5)
    mlstm_conv_b = 0.01 * nrm(ks[13], (DEPTH, 2 * GROUP))
    mlstm_norm_g = 1.0 + 0.02 * nrm(ks[14], (DEPTH, C_D))
    mem_norm_g = 1.0 + 0.02 * nrm(ks[15], (DEPTH, D_MODEL))
    w_mem_kv = nrm(ks[16], (DEPTH, D_MODEL, 2 * GROUP)) * (D_MODEL ** -0.5)
    xattn_qk_norm_g = 1.0 + 0.02 * nrm(ks[17], (DEPTH, 2, X_D))
    w_out = nrm(ks[18], (DEPTH, D_MIX, D_MODEL)) * (D_MIX ** -0.5)
    return {'x': x, 'mem': mem, 'positions': positions, 'norm_g': norm_g, 'w_in': w_in,
            'mlstm_gate_b': mlstm_gate_b, 'hgrn_lb_logits': hgrn_lb_logits,
            'hgrn_norm_g': hgrn_norm_g, 'diff_qk_norm_g': diff_qk_norm_g,
            'diff_lambda': diff_lambda, 'diff_subln_g': diff_subln_g,
            'mlstm_conv_w': mlstm_conv_w, 'mlstm_conv_b': mlstm_conv_b,
            'mlstm_norm_g': mlstm_norm_g, 'mem_norm_g': mem_norm_g, 'w_mem_kv': w_mem_kv,
            'xattn_qk_norm_g': xattn_qk_norm_g, 'w_out': w_out}


def reference(x, mem, positions, norm_g, w_in, mlstm_gate_b, hgrn_lb_logits, hgrn_norm_g,
              diff_qk_norm_g, diff_lambda, diff_subln_g, mlstm_conv_w, mlstm_conv_b,
              mlstm_norm_g, mem_norm_g, w_mem_kv, xattn_qk_norm_g, w_out):
    inv_freq = ROPE_THETA ** (-jnp.arange(0, ROPE_DIM, 2, dtype=jnp.float32) / ROPE_DIM)
    ang = positions.astype(jnp.float32)[..., None] * inv_freq
    cos = jnp.cos(ang)[:, :, None, None, :]
    sin = jnp.sin(ang)[:, :, None, None, :]
    sm = jax.nn.softmax(hgrn_lb_logits.astype(jnp.float32), axis=0)
    lower_bounds = jnp.cumsum(sm, axis=0) - sm[0]
    for l in range(DEPTH):
        x = hybrid_layer(x, mem, cos, sin, l, lower_bounds[l], norm_g[l], w_in[l],
                         mlstm_gate_b[l], hgrn_norm_g[l], diff_qk_norm_g[l], diff_lambda[l],
                         diff_subln_g[l], mlstm_conv_w[l], mlstm_conv_b[l], mlstm_norm_g[l],
                         mem_norm_g[l], w_mem_kv[l], xattn_qk_norm_g[l], w_out[l])
    return x
```

```python
import functools
import math

import jax
import jax.numpy as jnp
from jax import lax
from jax.experimental import pallas as pl
from jax.experimental.pallas import tpu as pltpu

F32 = jnp.float32
BF16 = jnp.bfloat16
HI = lax.Precision.HIGHEST

D_MODEL = 1024
N_MEM = 256
GROUP = 512
HEADS = 4
HD = 128
B_DQK = 64
ROPE_DIM = 16
ROPE_THETA = 500000.0
CONV_K = 4
EPS = 1e-6
NEG = -1e30
TINY = 1e-30

LANES = 128
N_MAIN = 15 * GROUP

(C_AQ, C_AF, C_AI, C_AG, C_BQ, C_BK, C_BV, C_BG,
 C_CQ, C_CK, C_CV, C_CO, C_CG, C_XQ, C_XG) = range(15)

IN_BM, IN_BN = 1024, 1536
A_BLK, A_CH, A_SUB = 512, 64, 16
C_BLK, C_CH = 512, 128
P_BM = 512
BQ = 512
M_BM = 512
T_BM = 1024

NT = (((1,), (1,)), ((), ()))
TN = (((0,), (0,)), ((), ()))


def _cp(sem, vmem_mib):
    return pltpu.CompilerParams(dimension_semantics=sem, vmem_limit_bytes=vmem_mib * 2 ** 20)


def _silu(x):
    return x * jax.nn.sigmoid(x)


def _log_sigmoid(x):
    return jnp.minimum(x, 0.0) - jnp.log1p(jnp.exp(-jnp.abs(x)))


def _rms(x, g):
    return x * lax.rsqrt(jnp.mean(x * x, axis=-1, keepdims=True) + EPS) * g


def _bdot(a, b):
    return jnp.dot(a, b, preferred_element_type=F32)


def _rope_body(pos_ref, invf_ref, cos_ref, sin_ref):
    ang = pos_ref[...].astype(F32) * invf_ref[...]
    lane = lax.broadcasted_iota(jnp.int32, ang.shape, 1) % B_DQK
    c, s = jnp.cos(ang), jnp.sin(ang)
    cos_ref[...] = jnp.where(lane < ROPE_DIM, c, 1.0)
    sin_ref[...] = jnp.where(lane < ROPE_DIM // 2, -s, jnp.where(lane < ROPE_DIM, s, 0.0))


def _rope_tables(positions):
    t = positions.size
    inv_freq = ROPE_THETA ** (-jnp.arange(0, ROPE_DIM, 2, dtype=F32) / ROPE_DIM)
    invf = jnp.tile(inv_freq, LANES // (ROPE_DIM // 2))[None, :]
    return pl.pallas_call(
        _rope_body,
        grid=(t // T_BM,),
        in_specs=[pl.BlockSpec((T_BM, 1), lambda i: (i, 0)),
                  pl.BlockSpec((1, LANES), lambda i: (0, 0))],
        out_specs=[pl.BlockSpec((T_BM, LANES), lambda i: (i, 0))] * 2,
        out_shape=[jax.ShapeDtypeStruct((t, LANES), F32)] * 2,
        compiler_params=_cp(("arbitrary",), 32),
        name="rope_tables",
    )(positions.reshape(t, 1), invf)


def _inproj_body(x_ref, g_ref, w_ref, wg_ref, wgt_ref, o_ref, gc_ref, gr_ref, h_ref):
    @pl.when(pl.program_id(1) == 0)
    def _():
        h = _rms(x_ref[...], g_ref[...]).astype(BF16)
        h_ref[...] = h
        gc_ref[...] = _bdot(h, wg_ref[...])
        gr = lax.dot_general(wgt_ref[...], h, NT, preferred_element_type=F32)
        for j in range(IN_BM // LANES):
            gr_ref[j] = gr[:8, j * LANES:(j + 1) * LANES]

    o_ref[...] = _bdot(h_ref[...], w_ref[...]).astype(o_ref.dtype)


def _inproj(x2, g, w_main, wg, wgt):
    t = x2.shape[0]
    return pl.pallas_call(
        _inproj_body,
        grid=(t // IN_BM, N_MAIN // IN_BN),
        in_specs=[pl.BlockSpec((IN_BM, D_MODEL), lambda i, j: (i, 0)),
                  pl.BlockSpec((1, D_MODEL), lambda i, j: (0, 0)),
                  pl.BlockSpec((D_MODEL, IN_BN), lambda i, j: (0, j)),
                  pl.BlockSpec((D_MODEL, LANES), lambda i, j: (0, 0)),
                  pl.BlockSpec((16, D_MODEL), lambda i, j: (0, 0))],
        out_specs=[pl.BlockSpec((IN_BM, IN_BN), lambda i, j: (i, j)),
                   pl.BlockSpec((IN_BM, LANES), lambda i, j: (i, 0)),
                   pl.BlockSpec((IN_BM // LANES, 8, LANES), lambda i, j: (i, 0, 0))],
        out_shape=[jax.ShapeDtypeStruct((t, N_MAIN), BF16),
                   jax.ShapeDtypeStruct((t, LANES), F32),
                   jax.ShapeDtypeStruct((t // LANES, 8, LANES), F32)],
        scratch_shapes=[pltpu.VMEM((IN_BM, D_MODEL), BF16)],
        compiler_params=_cp(("arbitrary", "arbitrary"), 48),
        name="in_proj",
    )(x2, g, w_main, wg, wgt)


def _hgrn_body(aq_ref, af_ref, ai_ref, ag_ref, lb_ref, ng_ref, tril_ref, y_ref, st_ref):
    @pl.when(pl.program_id(1) == 0)
    def _():
        st_ref[...] = jnp.zeros_like(st_ref)

    lb = lb_ref[...]
    oml = 1.0 - lb
    ng = ng_ref[...]
    nsub = A_CH // A_SUB
    row = lax.broadcasted_iota(jnp.int32, (A_CH, HD), 0)
    tsub = lax.broadcasted_iota(jnp.int32, (nsub, A_SUB, HD), 1)

    def chunk(c, carry):
        r0 = pl.multiple_of(c * A_CH, A_CH)
        rows = pl.ds(r0, A_CH)
        fa = af_ref[rows, :].astype(F32)
        q = _silu(aq_ref[rows, :].astype(F32)) * (HD ** -0.5)
        v = ai_ref[rows, :].astype(F32)
        gate = _silu(ag_ref[rows, :].astype(F32))
        f = lb + oml * jax.nn.sigmoid(fa)
        logf = jnp.log(jnp.maximum(f, TINY))
        ka = oml * jax.nn.sigmoid(-fa)
        b = jnp.dot(tril_ref[...], logf, precision=HI, preferred_element_type=F32)
        blast = b[A_CH - 1:A_CH, :]
        qdec = q * jnp.exp(b)
        kdec = ka * jnp.exp(blast - b)
        sdec = jnp.exp(blast)
        for h in range(HEADS):
            sl = slice(h * HD, (h + 1) * HD)
            qh, kh, vh, bh = q[:, sl], ka[:, sl], v[:, sl], b[:, sl]
            vhb = vh.astype(BF16)
            st = st_ref[h]
            o = lax.dot_general(qdec[:, sl].astype(BF16), st.astype(BF16), NT,
                                preferred_element_type=F32)
            parts = [jnp.zeros((A_SUB, A_CH), F32)]
            for i in range(1, nsub):
                lo = i * A_SUB
                bi = bh[lo - 1:lo, :]
                qi = (qh[lo:lo + A_SUB] * jnp.exp(bh[lo:lo + A_SUB] - bi)).astype(BF16)
                past = row < lo
                kp = jnp.where(past, kh * jnp.exp(jnp.where(past, bi - bh, 0.0)), 0.0)
                parts.append(lax.dot_general(qi, kp.astype(BF16), NT, preferred_element_type=F32))
            amat = jnp.concatenate(parts, axis=0).astype(BF16)
            o = o + _bdot(amat, vhb)
            q3 = qh.reshape(nsub, A_SUB, HD)
            k3 = kh.reshape(nsub, A_SUB, HD)
            b3 = bh.reshape(nsub, A_SUB, HD)
            v3 = vh.reshape(nsub, A_SUB, HD)
            acc = jnp.zeros((nsub, A_SUB, HD), F32)
            for s in range(A_SUB):
                ok = tsub >= s
                e = jnp.exp(jnp.where(ok, b3 - b3[:, s:s + 1, :], 0.0))
                wv = jnp.where(ok, q3 * k3[:, s:s + 1, :] * e, 0.0)
                a = jnp.sum(wv, axis=-1, keepdims=True)
                acc = acc + a * v3[:, s:s + 1, :]
            o = o + acc.reshape(A_CH, HD)
            st_ref[h] = st * sdec[:, sl] + lax.dot_general(
                vhb, kdec[:, sl].astype(BF16), TN, preferred_element_type=F32)
            y_ref[rows, sl] = (_rms(o, ng) * gate[:, sl]).astype(y_ref.dtype)
        return carry

    lax.fori_loop(0, A_BLK // A_CH, chunk, 0)


def _hgrn(proj, lb, ng, bsz, s):
    nb = s // A_BLK
    col = lambda cid: pl.BlockSpec((A_BLK, GROUP), lambda b, t, cid=cid: (b * nb + t, cid))
    tril = jnp.tril(jnp.ones((A_CH, A_CH), F32))
    return pl.pallas_call(
        _hgrn_body,
        grid=(bsz, nb),
        in_specs=[col(C_AQ), col(C_AF), col(C_AI), col(C_AG),
                  pl.BlockSpec((1, GROUP), lambda b, t: (0, 0)),
                  pl.BlockSpec((1, HD), lambda b, t: (0, 0)),
                  pl.BlockSpec((A_CH, A_CH), lambda b, t: (0, 0))],
        out_specs=pl.BlockSpec((A_BLK, GROUP), lambda b, t: (b * nb + t, 0)),
        out_shape=jax.ShapeDtypeStruct((bsz * s, GROUP), BF16),
        scratch_shapes=[pltpu.VMEM((HEADS, HD, HD), F32)],
        compiler_params=_cp(("arbitrary", "arbitrary"), 32),
        name="hgrn2",
    )(proj, proj, proj, proj, lb, ng, tril)


def _mlstm_body(cq_ref, ck_ref, cv_ref, co_ref, cg_ref, gcol_ref, grow_ref, cw_ref, cb_ref,
                gbl_ref, gbc_ref, ng_ref, tril_ref, triu_ref, y_ref,
                xq_s, xk_s, qc_s, kc_s, ca_s, m_s):
    @pl.when(pl.program_id(1) == 0)
    def _():
        xq_s[0:8, :] = jnp.zeros((8, GROUP), F32)
        xk_s[0:8, :] = jnp.zeros((8, GROUP), F32)
        ca_s[...] = jnp.zeros_like(ca_s)
        m_s[...] = jnp.zeros_like(m_s)

    xq_s[8:8 + C_BLK, :] = cq_ref[...].astype(F32)
    xk_s[8:8 + C_BLK, :] = ck_ref[...].astype(F32)
    cw = cw_ref[...]
    cb = cb_ref[...]
    accq = jnp.broadcast_to(cb[:, :GROUP], (C_BLK, GROUP))
    acck = jnp.broadcast_to(cb[:, GROUP:], (C_BLK, GROUP))
    for j in range(CONV_K):
        off = 8 - (CONV_K - 1) + j
        accq = accq + cw[j:j + 1, :GROUP] * xq_s[off:off + C_BLK, :]
        acck = acck + cw[j:j + 1, GROUP:] * xk_s[off:off + C_BLK, :]
    qc_s[...] = _silu(accq)
    kc_s[...] = _silu(acck) * (HD ** -0.5)
    xq_s[0:8, :] = xq_s[C_BLK:C_BLK + 8, :]
    xk_s[0:8, :] = xk_s[C_BLK:C_BLK + 8, :]

    ng = ng_ref[...]
    ri = lax.broadcasted_iota(jnp.int32, (C_CH, C_CH), 0)
    ci = lax.broadcasted_iota(jnp.int32, (C_CH, C_CH), 1)
    causal = ri >= ci
    ones_col = jnp.where(ci == 0, 1.0, 0.0)

    def chunk(c, carry):
        r0 = pl.multiple_of(c * C_CH, C_CH)
        rows = pl.ds(r0, C_CH)
        gc = gcol_ref[rows, :] + gbl_ref[...]
        bcols = jnp.dot(tril_ref[...], _log_sigmoid(gc), precision=HI, preferred_element_type=F32)
        gr = grow_ref[c] + gbc_ref[...]
        brows = jnp.dot(_log_sigmoid(gr), triu_ref[...], precision=HI, preferred_element_type=F32)
        for h in range(HEADS):
            sl = slice(h * HD, (h + 1) * HD)
            b_col = bcols[:, HEADS + h:HEADS + h + 1]
            li_col = gc[:, h:h + 1]
            b_row = brows[HEADS + h:HEADS + h + 1, :]
            li_row = gr[h:h + 1, :]
            m = m_s[h:h + 1, 0:1]
            ca = ca_s[h]
            qh = qc_s[rows, sl].astype(BF16)
            kh = kc_s[rows, sl].astype(BF16)
            vh = cv_ref[rows, sl].astype(F32)

            dm = jnp.where(causal, b_col - b_row + li_row, NEG)
            inter = b_col + m
            m_t = jnp.maximum(jnp.max(dm, axis=1, keepdims=True), inter)
            w = jnp.exp(dm - m_t)
            w_inter = jnp.exp(inter - m_t)
            sc = lax.dot_general(qh, kh, NT, preferred_element_type=F32)
            smat = (sc * w).astype(BF16)
            vaug = jnp.concatenate([vh, ones_col], axis=1).astype(BF16)
            tot = _bdot(smat, vaug) + w_inter * lax.dot_general(
                qh, ca.astype(BF16), NT, preferred_element_type=F32)
            num = tot[:, :HD]
            nq = tot[:, HD:HD + 1]
            hout = num / jnp.maximum(jnp.abs(nq), jnp.exp(-m_t))

            b_last = b_col[C_CH - 1:C_CH, :]
            g = b_last - b_col + li_col
            m_new = jnp.maximum(b_last + m, jnp.max(g, axis=0, keepdims=True))
            wg = jnp.exp(g - m_new)
            dec = jnp.exp(b_last + m - m_new)
            upd = jnp.concatenate([vh * wg, ones_col * wg], axis=1).astype(BF16)
            ca_s[h] = dec * ca + lax.dot_general(upd, kh, TN, preferred_element_type=F32)
            m_s[h:h + 1, :] = jnp.broadcast_to(m_new, (1, LANES))

            hc = jax.nn.sigmoid(co_ref[rows, sl].astype(F32)) * hout
            y_ref[rows, sl] = (_rms(hc, ng) * _silu(cg_ref[rows, sl].astype(F32))).astype(y_ref.dtype)
        return carry

    lax.fori_loop(0, C_BLK // C_CH, chunk, 0)


def _mlstm(proj, gcol, grow, cw, cb, gbl, gbc, ng, bsz, s):
    nb = s // C_BLK
    col = lambda cid: pl.BlockSpec((C_BLK, GROUP), lambda b, t, cid=cid: (b * nb + t, cid))
    full = lambda shp: pl.BlockSpec(shp, lambda b, t: (0,) * len(shp))
    tril = jnp.tril(jnp.ones((C_CH, C_CH), F32))
    return pl.pallas_call(
        _mlstm_body,
        grid=(bsz, nb),
        in_specs=[col(C_CQ), col(C_CK), col(C_CV), col(C_CO), col(C_CG),
                  pl.BlockSpec((C_BLK, LANES), lambda b, t: (b * nb + t, 0)),
                  pl.BlockSpec((C_BLK // LANES, 8, LANES), lambda b, t: (b * nb + t, 0, 0)),
                  full((CONV_K, 2 * GROUP)), full((1, 2 * GROUP)),
                  full((1, LANES)), full((8, 1)), full((1, HD)),
                  full((C_CH, C_CH)), full((C_CH, C_CH))],
        out_specs=pl.BlockSpec((C_BLK, GROUP), lambda b, t: (b * nb + t, 0)),
        out_shape=jax.ShapeDtypeStruct((bsz * s, GROUP), BF16),
        scratch_shapes=[pltpu.VMEM((C_BLK + 8, GROUP), F32), pltpu.VMEM((C_BLK + 8, GROUP), F32),
                        pltpu.VMEM((C_BLK, GROUP), F32), pltpu.VMEM((C_BLK, GROUP), F32),
                        pltpu.VMEM((HEADS, 2 * HD, HD), F32), pltpu.VMEM((8, LANES), F32)],
        compiler_params=_cp(("arbitrary", "arbitrary"), 32),
        name="mlstm",
    )(proj, proj, proj, proj, proj, gcol, grow, cw, cb, gbl, gbc, ng, tril, tril.T)


def _bpre_body(bq_ref, bk_ref, bv_ref, cos_ref, sin_ref, gq_ref, gk_ref, bd_ref,
               qt_ref, k_ref, vt_ref):
    reps = GROUP // LANES
    cosm = jnp.concatenate([cos_ref[...]] * reps, axis=1)
    sinm = jnp.concatenate([sin_ref[...]] * reps, axis=1)
    lane = lax.broadcasted_iota(jnp.int32, (P_BM, GROUP), 1)
    first_half = (lane % ROPE_DIM) < ROPE_DIM // 2
    bd = bd_ref[...]

    def prep(x, g):
        xsq = x * x
        hi = xsq.astype(BF16)
        lo = (xsq - hi.astype(F32)).astype(BF16)
        ms = _bdot(hi, bd) + _bdot(lo, bd)
        xn = x * lax.rsqrt(ms + EPS) * g
        partner = jnp.where(first_half,
                            pltpu.roll(xn, GROUP - ROPE_DIM // 2, 1),
                            pltpu.roll(xn, ROPE_DIM // 2, 1))
        return xn * cosm + partner * sinm

    q = prep(bq_ref[...].astype(F32), gq_ref[...]) * (B_DQK ** -0.5)
    k = prep(bk_ref[...].astype(F32), gk_ref[...])
    qt_ref[0] = q.T.astype(BF16)
    k_ref[...] = k.astype(BF16)
    vt_ref[0, 0] = bv_ref[...].astype(F32).T.astype(BF16)


def _bpre(proj, cosm, sinm, gq, gk, bsz, s):
    nb = s // P_BM
    t = bsz * s
    col = lambda cid: pl.BlockSpec((P_BM, GROUP), lambda i, cid=cid: (i, cid))
    seg = jnp.arange(GROUP) // B_DQK
    bd = (jnp.where(seg[:, None] == seg[None, :], 1.0 / B_DQK, 0.0)).astype(BF16)
    return pl.pallas_call(
        _bpre_body,
        grid=(t // P_BM,),
        in_specs=[col(C_BQ), col(C_BK), col(C_BV),
                  pl.BlockSpec((P_BM, LANES), lambda i: (i, 0)),
                  pl.BlockSpec((P_BM, LANES), lambda i: (i, 0)),
                  pl.BlockSpec((1, GROUP), lambda i: (0, 0)),
                  pl.BlockSpec((1, GROUP), lambda i: (0, 0)),
                  pl.BlockSpec((GROUP, GROUP), lambda i: (0, 0))],
        out_specs=[pl.BlockSpec((1, GROUP, P_BM), lambda i: (i // nb, 0, i % nb)),
                   pl.BlockSpec((P_BM, GROUP), lambda i: (i, 0)),
                   pl.BlockSpec((1, 1, GROUP, P_BM), lambda i: (i // nb, i % nb, 0, 0))],
        out_shape=[jax.ShapeDtypeStruct((bsz, GROUP, s), BF16),
                   jax.ShapeDtypeStruct((t, GROUP), BF16),
                   jax.ShapeDtypeStruct((bsz, s // P_BM, GROUP, P_BM), BF16)],
        compiler_params=_cp(("arbitrary",), 48),
        name="diff_prep",
    )(proj, proj, proj, cosm, sinm, gq, gk, bd)


def _flash_body(qt_ref, k_ref, vt_ref, bg_ref, lam_ref, sg_ref, y_ref,
                qz_s, m_s, l_s, acc_s, *, lam_init):
    qi = pl.program_id(2)
    qt = qt_ref[0]
    rowi = lax.broadcasted_iota(jnp.int32, (HD, BQ), 0)
    zero = jnp.zeros_like(qt)
    qz_s[0] = jnp.where(rowi < B_DQK, qt, zero)
    qz_s[1] = jnp.where(rowi >= B_DQK, qt, zero)
    m_s[...] = jnp.full_like(m_s, NEG)
    l_s[...] = jnp.zeros_like(l_s)
    acc_s[...] = jnp.zeros_like(acc_s)

    def step(kblk, vblk, mask):
        for c in range(2):
            s = _bdot(kblk, qz_s[c])
            if mask is not None:
                s = jnp.where(mask, s, NEG)
            m_old = m_s[c]
            m_new = jnp.maximum(m_old, jnp.max(s, axis=0, keepdims=True))
            alpha = jnp.exp(m_old - m_new)
            p = jnp.exp(s - m_new)
            l_s[c] = alpha * l_s[c] + jnp.sum(p, axis=0, keepdims=True)
            acc_s[c] = alpha * acc_s[c] + _bdot(vblk, p.astype(BF16))
            m_s[c] = m_new

    def body(kb, carry):
        r0 = pl.multiple_of(kb * BQ, BQ)
        step(k_ref[pl.ds(r0, BQ), :], vt_ref[0, kb], None)
        return carry

    lax.fori_loop(0, qi, body, 0)
    kidx = lax.broadcasted_iota(jnp.int32, (BQ, BQ), 0)
    qidx = lax.broadcasted_iota(jnp.int32, (BQ, BQ), 1)
    r0 = pl.multiple_of(qi * BQ, BQ)
    step(k_ref[pl.ds(r0, BQ), :], vt_ref[0, qi], kidx <= qidx)

    lp = lam_ref[...]
    lam = (jnp.exp(jnp.sum(lp[0:1] * lp[1:2], axis=1, keepdims=True))
           - jnp.exp(jnp.sum(lp[2:3] * lp[3:4], axis=1, keepdims=True)) + lam_init)
    ot = acc_s[0] / l_s[0] - lam * (acc_s[1] / l_s[1])
    o = ot.T
    y = _rms(o, sg_ref[...]) * (1.0 - lam_init) * _silu(bg_ref[...].astype(F32))
    y_ref[...] = y.astype(y_ref.dtype)


def _flash(qt, kk, vt, proj, lam_p, sg, bsz, s, lam_init):
    nq = s // BQ
    return pl.pallas_call(
        functools.partial(_flash_body, lam_init=lam_init),
        grid=(bsz, HEADS, nq),
        in_specs=[pl.BlockSpec((1, HD, BQ), lambda b, h, q: (b, h, q)),
                  pl.BlockSpec((s, HD), lambda b, h, q: (b, h)),
                  pl.BlockSpec((1, s // P_BM, HD, P_BM), lambda b, h, q: (b, 0, h, 0)),
                  pl.BlockSpec((BQ, HD), lambda b, h, q: (b * nq + q, C_BG * HEADS + h)),
                  pl.BlockSpec((4, B_DQK), lambda b, h, q: (0, 0)),
                  pl.BlockSpec((1, HD), lambda b, h, q: (0, 0))],
        out_specs=pl.BlockSpec((BQ, HD), lambda b, h, q: (b * nq + q, h)),
        out_shape=jax.ShapeDtypeStruct((bsz * s, GROUP), BF16),
        scratch_shapes=[pltpu.VMEM((2, HD, BQ), BF16), pltpu.VMEM((2, 1, BQ), F32),
                        pltpu.VMEM((2, 1, BQ), F32), pltpu.VMEM((2, HD, BQ), F32)],
        compiler_params=_cp(("arbitrary", "arbitrary", "arbitrary"), 48),
        name="diff_flash",
    )(qt, kk, vt, proj, lam_p, sg)


def _memkv_body(mem_ref, g_ref, w_ref, kg_ref, km_ref, vm_ref):
    mn = _rms(mem_ref[0], g_ref[...]).astype(BF16)
    kv = _bdot(mn, w_ref[...])
    for h in range(HEADS):
        sl = slice(h * HD, (h + 1) * HD)
        km_ref[0, :, sl] = _rms(kv[:, sl], kg_ref[...]).astype(BF16)
    vm_ref[0] = kv[:, GROUP:].astype(BF16)


def _memkv(mem, g, w, kg):
    bsz = mem.shape[0]
    return pl.pallas_call(
        _memkv_body,
        grid=(bsz,),
        in_specs=[pl.BlockSpec((1, N_MEM, D_MODEL), lambda b: (b, 0, 0)),
                  pl.BlockSpec((1, D_MODEL), lambda b: (0, 0)),
                  pl.BlockSpec((D_MODEL, 2 * GROUP), lambda b: (0, 0)),
                  pl.BlockSpec((1, HD), lambda b: (0, 0))],
        out_specs=[pl.BlockSpec((1, N_MEM, GROUP), lambda b: (b, 0, 0))] * 2,
        out_shape=[jax.ShapeDtypeStruct((bsz, N_MEM, GROUP), BF16)] * 2,
        compiler_params=_cp(("arbitrary",), 32),
        name="mem_kv",
    )(mem, g, w, kg)


def _merge_body(x_ref, ya_ref, yb_ref, yc_ref, xq_ref, xg_ref, km_ref, vm_ref, qg_ref, w_ref, o_ref):
    acc = x_ref[...]
    acc = acc + _bdot(ya_ref[...], w_ref[0:GROUP, :])
    acc = acc + _bdot(yb_ref[...], w_ref[GROUP:2 * GROUP, :])
    acc = acc + _bdot(yc_ref[...], w_ref[2 * GROUP:3 * GROUP, :])
    qg = qg_ref[...]
    for h in range(HEADS):
        sl = slice(h * HD, (h + 1) * HD)
        qn = (_rms(xq_ref[:, sl].astype(F32), qg) * (HD ** -0.5)).astype(BF16)
        s = lax.dot_general(qn, km_ref[0, :, sl], NT, preferred_element_type=F32)
        p = jnp.exp(s - jnp.max(s, axis=-1, keepdims=True))
        l = jnp.sum(p, axis=-1, keepdims=True)
        ox = _bdot(p.astype(BF16), vm_ref[0, :, sl]) / l
        yx = (ox * _silu(xg_ref[:, sl].astype(F32))).astype(BF16)
        acc = acc + _bdot(yx, w_ref[3 * GROUP + h * HD:3 * GROUP + (h + 1) * HD, :])
    o_ref[...] = acc


def _merge(x2, ya, yb, yc, proj, km, vm, qg, w_out, bsz, s):
    nb = s // M_BM
    t = bsz * s
    blk = lambda: pl.BlockSpec((M_BM, GROUP), lambda i: (i, 0))
    col = lambda cid: pl.BlockSpec((M_BM, GROUP), lambda i, cid=cid: (i, cid))
    return pl.pallas_call(
        _merge_body,
        grid=(t // M_BM,),
        in_specs=[pl.BlockSpec((M_BM, D_MODEL), lambda i: (i, 0)),
                  blk(), blk(), blk(), col(C_XQ), col(C_XG),
                  pl.BlockSpec((1, N_MEM, GROUP), lambda i: (i // nb, 0, 0)),
                  pl.BlockSpec((1, N_MEM, GROUP), lambda i: (i // nb, 0, 0)),
                  pl.BlockSpec((1, HD), lambda i: (0, 0)),
                  pl.BlockSpec((4 * GROUP, D_MODEL), lambda i: (0, 0))],
        out_specs=pl.BlockSpec((M_BM, D_MODEL), lambda i: (i, 0)),
        out_shape=jax.ShapeDtypeStruct((t, D_MODEL), F32),
        compiler_params=_cp(("arbitrary",), 48),
        name="xattn_out_proj",
    )(x2, ya, yb, yc, proj, proj, km, vm, qg, w_out)


def _layer(x2, mem, cosm, sinm, layer_idx, lb, norm_g, w_in, mlstm_gate_b, hgrn_norm_g,
           diff_qk_norm_g, diff_lambda, diff_subln_g, mlstm_conv_w, mlstm_conv_b,
           mlstm_norm_g, mem_norm_g, w_mem_kv, xattn_qk_norm_g, w_out, bsz, s):
    g12 = 12 * GROUP
    w_main = jnp.concatenate([w_in[:, :g12], w_in[:, g12 + 2 * HEADS:]], axis=1).astype(BF16)
    w_gate = w_in[:, g12:g12 + 2 * HEADS]
    wg = jnp.pad(w_gate, ((0, 0), (0, LANES - 2 * HEADS))).astype(BF16)
    wgt = jnp.pad(w_gate.T, ((0, 8), (0, 0))).astype(BF16)
    proj, gcol, grow = _inproj(x2, norm_g[None, :], w_main, wg, wgt)

    ya = _hgrn(proj, lb[None, :], hgrn_norm_g[None, :], bsz, s)

    gq = jnp.tile(diff_qk_norm_g[0], GROUP // B_DQK)[None, :]
    gk = jnp.tile(diff_qk_norm_g[1], GROUP // B_DQK)[None, :]
    qt, kk, vt = _bpre(proj, cosm, sinm, gq, gk, bsz, s)
    lam_init = 0.8 - 0.6 * math.exp(-0.3 * layer_idx)
    yb = _flash(qt, kk, vt, proj, diff_lambda, diff_subln_g[None, :], bsz, s, lam_init)

    gbl = jnp.pad(mlstm_gate_b, (0, LANES - 2 * HEADS))[None, :]
    yc = _mlstm(proj, gcol, grow, mlstm_conv_w, mlstm_conv_b[None, :], gbl,
                mlstm_gate_b[:, None], mlstm_norm_g[None, :], bsz, s)

    km, vm = _memkv(mem, mem_norm_g[None, :], w_mem_kv.astype(BF16), xattn_qk_norm_g[1][None, :])
    return _merge(x2, ya, yb, yc, proj, km, vm, xattn_qk_norm_g[0][None, :],
                  w_out.astype(BF16), bsz, s)


def kernel(x, mem, positions, norm_g, w_in, mlstm_gate_b, hgrn_lb_logits, hgrn_norm_g,
           diff_qk_norm_g, diff_lambda, diff_subln_g, mlstm_conv_w, mlstm_conv_b,
           mlstm_norm_g, mem_norm_g, w_mem_kv, xattn_qk_norm_g, w_out):
    bsz, s, d = x.shape
    depth = norm_g.shape[0]
    assert d == D_MODEL and s % IN_BM == 0 and mem.shape[1] == N_MEM
    cosm, sinm = _rope_tables(positions)
    sm = jax.nn.softmax(hgrn_lb_logits.astype(F32), axis=0)
    lower_bounds = jnp.cumsum(sm, axis=0) - sm[0]
    x2 = x.reshape(bsz * s, d)
    for l in range(depth):
        x2 = _layer(x2, mem, cosm, sinm, l, lower_bounds[l], norm_g[l], w_in[l], mlstm_gate_b[l],
                    hgrn_norm_g[l], diff_qk_norm_g[l], diff_lambda[l], diff_subln_g[l],
                    mlstm_conv_w[l], mlstm_conv_b[l], mlstm_norm_g[l], mem_norm_g[l],
                    w_mem_kv[l], xattn_qk_norm_g[l], w_out[l], bsz, s)
    return x2.reshape(bsz, s, d)
```

```python
import functools
import math

import jax
import jax.numpy as jnp
from jax import lax
from jax.experimental import pallas as pl
from jax.experimental.pallas import tpu as pltpu

F32 = jnp.float32
BF16 = jnp.bfloat16
HI = lax.Precision.HIGHEST

D_MODEL = 1024
N_MEM = 256
GROUP = 512
HEADS = 4
HD = 128
B_DQK = 64
ROPE_DIM = 16
ROPE_THETA = 500000.0
CONV_K = 4
EPS = 1e-6
NEG = -1e30
TINY = 1e-30
LOG2E = math.log2(math.e)
MAX_UNSHIFTED_LOG2_SCORE = 100.0

LANES = 128
N_MAIN = 15 * GROUP

(C_AQ, C_AF, C_AI, C_AG, C_BQ, C_BK, C_BV, C_BG,
 C_CQ, C_CK, C_CV, C_CO, C_CG, C_XQ, C_XG) = range(15)

IN_BM, IN_BN = 1024, 1536
A_BLK, A_CH, A_SUB = 512, 64, 16
C_BLK, C_CH = 512, 128
P_BM = 512
BQ = 1024
BK = 1024
BKD = 512
M_BM = 512
T_BM = 1024

NT = (((1,), (1,)), ((), ()))
TN = (((0,), (0,)), ((), ()))


def _cp(sem, vmem_mib):
    return pltpu.CompilerParams(dimension_semantics=sem, vmem_limit_bytes=vmem_mib * 2 ** 20)


def _silu(x):
    return x * jax.nn.sigmoid(x)


def _log_sigmoid(x):
    return jnp.minimum(x, 0.0) - jnp.log1p(jnp.exp(-jnp.abs(x)))


def _rms(x, g):
    return x * lax.rsqrt(jnp.mean(x * x, axis=-1, keepdims=True) + EPS) * g


def _bdot(a, b):
    return jnp.dot(a, b, preferred_element_type=F32)


def _rope_body(pos_ref, invf_ref, cos_ref, sin_ref):
    ang = pos_ref[...].astype(F32) * invf_ref[...]
    lane = lax.broadcasted_iota(jnp.int32, ang.shape, 1) % B_DQK
    c, s = jnp.cos(ang), jnp.sin(ang)
    cos_ref[...] = jnp.where(lane < ROPE_DIM, c, 1.0)
    sin_ref[...] = jnp.where(lane < ROPE_DIM // 2, -s, jnp.where(lane < ROPE_DIM, s, 0.0))


def _rope_tables(positions):
    t = positions.size
    inv_freq = ROPE_THETA ** (-jnp.arange(0, ROPE_DIM, 2, dtype=F32) / ROPE_DIM)
    invf = jnp.tile(inv_freq, LANES // (ROPE_DIM // 2))[None, :]
    return pl.pallas_call(
        _rope_body,
        grid=(t // T_BM,),
        in_specs=[pl.BlockSpec((T_BM, 1), lambda i: (i, 0)),
                  pl.BlockSpec((1, LANES), lambda i: (0, 0))],
        out_specs=[pl.BlockSpec((T_BM, LANES), lambda i: (i, 0))] * 2,
        out_shape=[jax.ShapeDtypeStruct((t, LANES), F32)] * 2,
        compiler_params=_cp(("arbitrary",), 32),
        name="rope_tables",
    )(positions.reshape(t, 1), invf)


def _inproj_body(x_ref, g_ref, w_ref, wg_ref, wgt_ref, o_ref, gc_ref, gr_ref, h_ref):
    @pl.when(pl.program_id(1) == 0)
    def _():
        h = _rms(x_ref[...], g_ref[...]).astype(BF16)
        h_ref[...] = h
        gc_ref[...] = _bdot(h, wg_ref[...])
        gr = lax.dot_general(wgt_ref[...], h, NT, preferred_element_type=F32)
        for j in range(IN_BM // LANES):
            gr_ref[j] = gr[:8, j * LANES:(j + 1) * LANES]

    o_ref[...] = _bdot(h_ref[...], w_ref[...]).astype(o_ref.dtype)


def _inproj(x2, g, w_main, wg, wgt):
    t = x2.shape[0]
    return pl.pallas_call(
        _inproj_body,
        grid=(t // IN_BM, N_MAIN // IN_BN),
        in_specs=[pl.BlockSpec((IN_BM, D_MODEL), lambda i, j: (i, 0)),
                  pl.BlockSpec((1, D_MODEL), lambda i, j: (0, 0)),
                  pl.BlockSpec((D_MODEL, IN_BN), lambda i, j: (0, j)),
                  pl.BlockSpec((D_MODEL, LANES), lambda i, j: (0, 0)),
                  pl.BlockSpec((16, D_MODEL), lambda i, j: (0, 0))],
        out_specs=[pl.BlockSpec((IN_BM, IN_BN), lambda i, j: (i, j)),
                   pl.BlockSpec((IN_BM, LANES), lambda i, j: (i, 0)),
                   pl.BlockSpec((IN_BM // LANES, 8, LANES), lambda i, j: (i, 0, 0))],
        out_shape=[jax.ShapeDtypeStruct((t, N_MAIN), BF16),
                   jax.ShapeDtypeStruct((t, LANES), F32),
                   jax.ShapeDtypeStruct((t // LANES, 8, LANES), F32)],
        scratch_shapes=[pltpu.VMEM((IN_BM, D_MODEL), BF16)],
        compiler_params=_cp(("arbitrary", "arbitrary"), 48),
        name="in_proj",
    )(x2, g, w_main, wg, wgt)


def _hgrn_body(aq_ref, af_ref, ai_ref, ag_ref, lb_ref, ng_ref, tril_ref, y_ref, st_ref):
    @pl.when(pl.program_id(1) == 0)
    def _():
        st_ref[...] = jnp.zeros_like(st_ref)

    lb = lb_ref[...]
    oml = 1.0 - lb
    ng = ng_ref[...]
    nsub = A_CH // A_SUB
    row = lax.broadcasted_iota(jnp.int32, (A_CH, HD), 0)
    tsub = lax.broadcasted_iota(jnp.int32, (nsub, A_SUB, HD), 1)

    def chunk(c, carry):
        r0 = pl.multiple_of(c * A_CH, A_CH)
        rows = pl.ds(r0, A_CH)
        fa = af_ref[rows, :].astype(F32)
        q = _silu(aq_ref[rows, :].astype(F32)) * (HD ** -0.5)
        v = ai_ref[rows, :].astype(F32)
        gate = _silu(ag_ref[rows, :].astype(F32))
        f = lb + oml * jax.nn.sigmoid(fa)
        logf = jnp.log(jnp.maximum(f, TINY))
        ka = oml * jax.nn.sigmoid(-fa)
        b = jnp.dot(tril_ref[...], logf, precision=HI, preferred_element_type=F32)
        blast = b[A_CH - 1:A_CH, :]
        qdec = q * jnp.exp(b)
        kdec = ka * jnp.exp(blast - b)
        sdec = jnp.exp(blast)
        for h in range(HEADS):
            sl = slice(h * HD, (h + 1) * HD)
            qh, kh, vh, bh = q[:, sl], ka[:, sl], v[:, sl], b[:, sl]
            vhb = vh.astype(BF16)
            st = st_ref[h]
            o = lax.dot_general(qdec[:, sl].astype(BF16), st.astype(BF16), NT,
                                preferred_element_type=F32)
            parts = [jnp.zeros((A_SUB, A_CH), F32)]
            for i in range(1, nsub):
                lo = i * A_SUB
                bi = bh[lo - 1:lo, :]
                qi = (qh[lo:lo + A_SUB] * jnp.exp(bh[lo:lo + A_SUB] - bi)).astype(BF16)
                past = row < lo
                kp = jnp.where(past, kh * jnp.exp(jnp.where(past, bi - bh, 0.0)), 0.0)
                parts.append(lax.dot_general(qi, kp.astype(BF16), NT, preferred_element_type=F32))
            amat = jnp.concatenate(parts, axis=0).astype(BF16)
            o = o + _bdot(amat, vhb)
            q3 = qh.reshape(nsub, A_SUB, HD)
            k3 = kh.reshape(nsub, A_SUB, HD)
            b3 = bh.reshape(nsub, A_SUB, HD)
            v3 = vh.reshape(nsub, A_SUB, HD)
            acc = jnp.zeros((nsub, A_SUB, HD), F32)
            for s in range(A_SUB):
                ok = tsub >= s
                e = jnp.exp(jnp.where(ok, b3 - b3[:, s:s + 1, :], 0.0))
                wv = jnp.where(ok, q3 * k3[:, s:s + 1, :] * e, 0.0)
                a = jnp.sum(wv, axis=-1, keepdims=True)
                acc = acc + a * v3[:, s:s + 1, :]
            o = o + acc.reshape(A_CH, HD)
            st_ref[h] = st * sdec[:, sl] + lax.dot_general(
                vhb, kdec[:, sl].astype(BF16), TN, preferred_element_type=F32)
            y_ref[rows, sl] = (_rms(o, ng) * gate[:, sl]).astype(y_ref.dtype)
        return carry

    lax.fori_loop(0, A_BLK // A_CH, chunk, 0)


def _hgrn(proj, lb, ng, bsz, s):
    nb = s // A_BLK
    col = lambda cid: pl.BlockSpec((A_BLK, GROUP), lambda b, t, cid=cid: (b * nb + t, cid))
    tril = jnp.tril(jnp.ones((A_CH, A_CH), F32))
    return pl.pallas_call(
        _hgrn_body,
        grid=(bsz, nb),
        in_specs=[col(C_AQ), col(C_AF), col(C_AI), col(C_AG),
                  pl.BlockSpec((1, GROUP), lambda b, t: (0, 0)),
                  pl.BlockSpec((1, HD), lambda b, t: (0, 0)),
                  pl.BlockSpec((A_CH, A_CH), lambda b, t: (0, 0))],
        out_specs=pl.BlockSpec((A_BLK, GROUP), lambda b, t: (b * nb + t, 0)),
        out_shape=jax.ShapeDtypeStruct((bsz * s, GROUP), BF16),
        scratch_shapes=[pltpu.VMEM((HEADS, HD, HD), F32)],
        compiler_params=_cp(("arbitrary", "arbitrary"), 32),
        name="hgrn2",
    )(proj, proj, proj, proj, lb, ng, tril)


def _mlstm_body(cq_ref, ck_ref, cv_ref, co_ref, cg_ref, gcol_ref, grow_ref, cw_ref, cb_ref,
                gbl_ref, gbc_ref, ng_ref, tril_ref, triu_ref, y_ref,
                xq_s, xk_s, qc_s, kc_s, ca_s, m_s):
    @pl.when(pl.program_id(1) == 0)
    def _():
        xq_s[0:8, :] = jnp.zeros((8, GROUP), F32)
        xk_s[0:8, :] = jnp.zeros((8, GROUP), F32)
        ca_s[...] = jnp.zeros_like(ca_s)
        m_s[...] = jnp.zeros_like(m_s)

    xq_s[8:8 + C_BLK, :] = cq_ref[...].astype(F32)
    xk_s[8:8 + C_BLK, :] = ck_ref[...].astype(F32)
    cw = cw_ref[...]
    cb = cb_ref[...]
    accq = jnp.broadcast_to(cb[:, :GROUP], (C_BLK, GROUP))
    acck = jnp.broadcast_to(cb[:, GROUP:], (C_BLK, GROUP))
    for j in range(CONV_K):
        off = 8 - (CONV_K - 1) + j
        accq = accq + cw[j:j + 1, :GROUP] * xq_s[off:off + C_BLK, :]
        acck = acck + cw[j:j + 1, GROUP:] * xk_s[off:off + C_BLK, :]
    qc_s[...] = _silu(accq)
    kc_s[...] = _silu(acck) * (HD ** -0.5)
    xq_s[0:8, :] = xq_s[C_BLK:C_BLK + 8, :]
    xk_s[0:8, :] = xk_s[C_BLK:C_BLK + 8, :]

    ng = ng_ref[...]
    ri = lax.broadcasted_iota(jnp.int32, (C_CH, C_CH), 0)
    ci = lax.broadcasted_iota(jnp.int32, (C_CH, C_CH), 1)
    causal = ri >= ci
    ones_col = jnp.where(ci == 0, 1.0, 0.0)

    def chunk(c, carry):
        r0 = pl.multiple_of(c * C_CH, C_CH)
        rows = pl.ds(r0, C_CH)
        gc = gcol_ref[rows, :] + gbl_ref[...]
        bcols = jnp.dot(tril_ref[...], _log_sigmoid(gc), precision=HI, preferred_element_type=F32)
        gr = grow_ref[c] + gbc_ref[...]
        brows = jnp.dot(_log_sigmoid(gr), triu_ref[...], precision=HI, preferred_element_type=F32)
        for h in range(HEADS):
            sl = slice(h * HD, (h + 1) * HD)
            b_col = bcols[:, HEADS + h:HEADS + h + 1]
            li_col = gc[:, h:h + 1]
            b_row = brows[HEADS + h:HEADS + h + 1, :]
            li_row = gr[h:h + 1, :]
            m = m_s[h:h + 1, 0:1]
            ca = ca_s[h]
            qh = qc_s[rows, sl].astype(BF16)
            kh = kc_s[rows, sl].astype(BF16)
            vh = cv_ref[rows, sl].astype(F32)

            dm = jnp.where(causal, b_col - b_row + li_row, NEG)
            inter = b_col + m
            m_t = jnp.maximum(jnp.max(dm, axis=1, keepdims=True), inter)
            w = jnp.exp(dm - m_t)
            w_inter = jnp.exp(inter - m_t)
            sc = lax.dot_general(qh, kh, NT, preferred_element_type=F32)
            smat = (sc * w).astype(BF16)
            vaug = jnp.concatenate([vh, ones_col], axis=1).astype(BF16)
            tot = _bdot(smat, vaug) + w_inter * lax.dot_general(
                qh, ca.astype(BF16), NT, preferred_element_type=F32)
            num = tot[:, :HD]
            nq = tot[:, HD:HD + 1]
            hout = num / jnp.maximum(jnp.abs(nq), jnp.exp(-m_t))

            b_last = b_col[C_CH - 1:C_CH, :]
            g = b_last - b_col + li_col
            m_new = jnp.maximum(b_last + m, jnp.max(g, axis=0, keepdims=True))
            wg = jnp.exp(g - m_new)
            dec = jnp.exp(b_last + m - m_new)
            upd = jnp.concatenate([vh * wg, ones_col * wg], axis=1).astype(BF16)
            ca_s[h] = dec * ca + lax.dot_general(upd, kh, TN, preferred_element_type=F32)
            m_s[h:h + 1, :] = jnp.broadcast_to(m_new, (1, LANES))

            hc = jax.nn.sigmoid(co_ref[rows, sl].astype(F32)) * hout
            y_ref[rows, sl] = (_rms(hc, ng) * _silu(cg_ref[rows, sl].astype(F32))).astype(y_ref.dtype)
        return carry

    lax.fori_loop(0, C_BLK // C_CH, chunk, 0)


def _mlstm(proj, gcol, grow, cw, cb, gbl, gbc, ng, bsz, s):
    nb = s // C_BLK
    col = lambda cid: pl.BlockSpec((C_BLK, GROUP), lambda b, t, cid=cid: (b * nb + t, cid))
    full = lambda shp: pl.BlockSpec(shp, lambda b, t: (0,) * len(shp))
    tril = jnp.tril(jnp.ones((C_CH, C_CH), F32))
    return pl.pallas_call(
        _mlstm_body,
        grid=(bsz, nb),
        in_specs=[col(C_CQ), col(C_CK), col(C_CV), col(C_CO), col(C_CG),
                  pl.BlockSpec((C_BLK, LANES), lambda b, t: (b * nb + t, 0)),
                  pl.BlockSpec((C_BLK // LANES, 8, LANES), lambda b, t: (b * nb + t, 0, 0)),
                  full((CONV_K, 2 * GROUP)), full((1, 2 * GROUP)),
                  full((1, LANES)), full((8, 1)), full((1, HD)),
                  full((C_CH, C_CH)), full((C_CH, C_CH))],
        out_specs=pl.BlockSpec((C_BLK, GROUP), lambda b, t: (b * nb + t, 0)),
        out_shape=jax.ShapeDtypeStruct((bsz * s, GROUP), BF16),
        scratch_shapes=[pltpu.VMEM((C_BLK + 8, GROUP), F32), pltpu.VMEM((C_BLK + 8, GROUP), F32),
                        pltpu.VMEM((C_BLK, GROUP), F32), pltpu.VMEM((C_BLK, GROUP), F32),
                        pltpu.VMEM((HEADS, 2 * HD, HD), F32), pltpu.VMEM((8, LANES), F32)],
        compiler_params=_cp(("arbitrary", "arbitrary"), 32),
        name="mlstm",
    )(proj, proj, proj, proj, proj, gcol, grow, cw, cb, gbl, gbc, ng, tril, tril.T)


def _bpre_body(bq_ref, bk_ref, bv_ref, cos_ref, sin_ref, gq_ref, gk_ref, bd_ref,
               qt_ref, k_ref, vt_ref):
    reps = GROUP // LANES
    cosm = jnp.concatenate([cos_ref[...]] * reps, axis=1)
    sinm = jnp.concatenate([sin_ref[...]] * reps, axis=1)
    lane = lax.broadcasted_iota(jnp.int32, (P_BM, GROUP), 1)
    first_half = (lane % ROPE_DIM) < ROPE_DIM // 2
    bd = bd_ref[...]

    def prep(x, g):
        xsq = x * x
        hi = xsq.astype(BF16)
        lo = (xsq - hi.astype(F32)).astype(BF16)
        ms = _bdot(hi, bd) + _bdot(lo, bd)
        xn = x * lax.rsqrt(ms + EPS) * g
        partner = jnp.where(first_half,
                            pltpu.roll(xn, GROUP - ROPE_DIM // 2, 1),
                            pltpu.roll(xn, ROPE_DIM // 2, 1))
        return xn * cosm + partner * sinm

    q = prep(bq_ref[...].astype(F32), gq_ref[...]) * (B_DQK ** -0.5 * LOG2E)
    k = prep(bk_ref[...].astype(F32), gk_ref[...])
    qt_ref[0] = q.T.astype(BF16)
    k_ref[...] = k.astype(BF16)
    vt_ref[0, 0] = bv_ref[...].astype(F32).T.astype(BF16)


def _bpre(proj, cosm, sinm, gq, gk, bsz, s):
    nb = s // P_BM
    ppk = BK // P_BM
    t = bsz * s
    col = lambda cid: pl.BlockSpec((P_BM, GROUP), lambda i, cid=cid: (i, cid))
    seg =jnp.arange(GROUP) // B_DQK
    bd = (jnp.where(seg[:, None] == seg[None, :], 1.0 / B_DQK, 0.0)).astype(BF16)
    return pl.pallas_call(
        _bpre_body,
        grid=(t // P_BM,),
        in_specs=[col(C_BQ), col(C_BK), col(C_BV),
                  pl.BlockSpec((P_BM, LANES), lambda i: (i, 0)),
                  pl.BlockSpec((P_BM, LANES), lambda i: (i, 0)),
                  pl.BlockSpec((1, GROUP), lambda i: (0, 0)),
                  pl.BlockSpec((1, GROUP), lambda i: (0, 0)),
                  pl.BlockSpec((GROUP, GROUP), lambda i: (0, 0))],
        out_specs=[pl.BlockSpec((1, GROUP, P_BM), lambda i: (i // nb, 0, i % nb)),
                   pl.BlockSpec((P_BM, GROUP), lambda i: (i, 0)),
                   pl.BlockSpec((1, 1, GROUP, P_BM),
                                lambda i: (i // nb, (i % nb) // ppk, 0, (i % nb) % ppk))],
        out_shape=[jax.ShapeDtypeStruct((bsz, GROUP, s), BF16),
                   jax.ShapeDtypeStruct((t, GROUP), BF16),
                   jax.ShapeDtypeStruct((bsz, s // BK, GROUP, BK), BF16)],
        compiler_params=_cp(("arbitrary",), 48),
        name="diff_prep",
    )(proj, proj, proj, cosm, sinm, gq, gk, bd)


def _flash_body(qt_ref, k_ref, vt_ref, bg_ref, lam_ref, sg_ref, y_ref,
                qz_s, m_s, l_s, acc_s, *, lam_init, online_max):
    qi = pl.program_id(2)
    qt = qt_ref[0]
    rowi = lax.broadcasted_iota(jnp.int32, (HD, BQ), 0)
    zero = jnp.zeros_like(qt)
    qz_s[0] = jnp.where(rowi < B_DQK, qt, zero)
    qz_s[1] = jnp.where(rowi >= B_DQK, qt, zero)
    m_s[...] = jnp.full_like(m_s, NEG)
    l_s[...] = jnp.zeros_like(l_s)
    acc_s[...] = jnp.zeros_like(acc_s)

    def step(kblk, vblk, q0, mask):
        qs = slice(q0, BQ)
        for c in range(2):
            s = _bdot(kblk, qz_s[c, :, qs])
            if mask is not None:
                s = jnp.where(mask, s, NEG)
            if online_max:
                m_old = m_s[c, :, qs]
                m_new = jnp.maximum(m_old, jnp.max(s, axis=0, keepdims=True))
                alpha = jnp.exp2(m_old - m_new)
                p = jnp.exp2(s - m_new)
                l_s[c, :, qs] = alpha * l_s[c, :, qs] + jnp.sum(p, axis=0, keepdims=True)
                acc_s[c, :, qs] = alpha * acc_s[c, :, qs] + _bdot(vblk, p.astype(BF16))
                m_s[c, :, qs] = m_new
            else:
                p = jnp.exp2(s)
                l_s[c, :, qs] = l_s[c, :, qs] + jnp.sum(p, axis=0, keepdims=True)
                acc_s[c, :, qs] = acc_s[c, :, qs] + _bdot(vblk, p.astype(BF16))

    def body(kb, carry):
        r0 = pl.multiple_of(kb * BK, BK)
        step(k_ref[pl.ds(r0, BK), :], vt_ref[0, kb], 0, None)
        return carry

    lax.fori_loop(0, qi, body, 0)
    r0 = pl.multiple_of(qi * BK, BK)
    vdiag = vt_ref[0, qi]
    for d in range(BK // BKD):
        q0 = d * BKD
        kidx = lax.broadcasted_iota(jnp.int32, (BKD, BQ - q0), 0)
        qidx = lax.broadcasted_iota(jnp.int32, (BKD, BQ - q0), 1)
        rows = pl.ds(pl.multiple_of(r0 + q0, BKD), BKD)
        step(k_ref[rows, :], vdiag[:, q0:q0 + BKD], q0, kidx <= qidx)

    lp = lam_ref[...]
    lam = (jnp.exp(jnp.sum(lp[0:1] * lp[1:2], axis=1, keepdims=True))
           - jnp.exp(jnp.sum(lp[2:3] * lp[3:4], axis=1, keepdims=True)) + lam_init)
    ot = acc_s[0] / l_s[0] - lam * (acc_s[1] / l_s[1])
    o = ot.T
    y = _rms(o, sg_ref[...]) * (1.0 - lam_init) * _silu(bg_ref[...].astype(F32))
    y_ref[...] = y.astype(y_ref.dtype)


def _flash_call(qt, kk, vt, proj, lam_p, sg, *, bsz, s, lam_init, online_max):
    nq = s // BQ
    return pl.pallas_call(
        functools.partial(_flash_body, lam_init=lam_init, online_max=online_max),
        grid=(bsz, HEADS, nq),
        in_specs=[pl.BlockSpec((1, HD, BQ), lambda b, h, q: (b, h, q)),
                  pl.BlockSpec((s, HD), lambda b, h, q: (b, h)),
                  pl.BlockSpec((1, s // BK, HD, BK), lambda b, h, q: (b, 0, h, 0)),
                  pl.BlockSpec((BQ, HD), lambda b, h, q: (b * nq + q, C_BG * HEADS + h)),
                  pl.BlockSpec((4, B_DQK), lambda b, h, q: (0, 0)),
                  pl.BlockSpec((1, HD), lambda b, h, q: (0, 0))],
        out_specs=pl.BlockSpec((BQ, HD), lambda b, h, q: (b * nq + q, h)),
        out_shape=jax.ShapeDtypeStruct((bsz * s, GROUP), BF16),
        scratch_shapes=[pltpu.VMEM((2, HD, BQ), BF16), pltpu.VMEM((2, 1, BQ), F32),
                        pltpu.VMEM((2, 1, BQ), F32), pltpu.VMEM((2, HD, BQ), F32)],
        compiler_params=_cp(("arbitrary", "arbitrary", "arbitrary"), 48),
        name="diff_flash_online" if online_max else "diff_flash",
    )(qt, kk, vt, proj, lam_p, sg)


def _flash(qt, kk, vt, proj, lam_p, sg, qk_g, bsz, s, lam_init):
    bound = (B_DQK ** 0.5) * LOG2E * jnp.max(jnp.abs(qk_g[0])) * jnp.max(jnp.abs(qk_g[1]))
    args = (qt, kk, vt, proj, lam_p, sg)
    call = functools.partial(_flash_call, bsz=bsz, s=s, lam_init=lam_init)
    return lax.cond(bound <= MAX_UNSHIFTED_LOG2_SCORE - math.log2(s),
                    functools.partial(call, online_max=False),
                    functools.partial(call, online_max=True), *args)


def _memkv_body(mem_ref, g_ref, w_ref, kg_ref, km_ref, vm_ref):
    mn = _rms(mem_ref[0], g_ref[...]).astype(BF16)
    kv = _bdot(mn, w_ref[...])
    for h in range(HEADS):
        sl = slice(h * HD, (h + 1) * HD)
        km_ref[0, :, sl] = _rms(kv[:, sl], kg_ref[...]).astype(BF16)
    vm_ref[0] = kv[:, GROUP:].astype(BF16)


def _memkv(mem, g, w, kg):
    bsz = mem.shape[0]
    return pl.pallas_call(
        _memkv_body,
        grid=(bsz,),
        in_specs=[pl.BlockSpec((1, N_MEM, D_MODEL), lambda b: (b, 0, 0)),
                  pl.BlockSpec((1, D_MODEL), lambda b: (0, 0)),
                  pl.BlockSpec((D_MODEL, 2 * GROUP), lambda b: (0, 0)),
                  pl.BlockSpec((1, HD), lambda b: (0, 0))],
        out_specs=[pl.BlockSpec((1, N_MEM, GROUP), lambda b: (b, 0, 0))] * 2,
        out_shape=[jax.ShapeDtypeStruct((bsz, N_MEM, GROUP), BF16)] * 2,
        compiler_params=_cp(("arbitrary",), 32),
        name="mem_kv",
    )(mem, g, w, kg)


def _merge_body(x_ref, ya_ref, yb_ref, yc_ref, xq_ref, xg_ref, km_ref, vm_ref, qg_ref, w_ref, o_ref):
    acc = x_ref[...]
    acc = acc + _bdot(ya_ref[...], w_ref[0:GROUP, :])
    acc = acc + _bdot(yb_ref[...], w_ref[GROUP:2 * GROUP, :])
    acc = acc + _bdot(yc_ref[...], w_ref[2 * GROUP:3 * GROUP, :])
    qg = qg_ref[...]
    for h in range(HEADS):
        sl = slice(h * HD, (h + 1) * HD)
        qn = (_rms(xq_ref[:, sl].astype(F32), qg) * (HD ** -0.5)).astype(BF16)
        s = lax.dot_general(qn, km_ref[0, :, sl], NT, preferred_element_type=F32)
        p = jnp.exp(s - jnp.max(s, axis=-1, keepdims=True))
        l = jnp.sum(p, axis=-1, keepdims=True)
        ox = _bdot(p.astype(BF16), vm_ref[0, :, sl]) / l
        yx = (ox * _silu(xg_ref[:, sl].astype(F32))).astype(BF16)
        acc = acc + _bdot(yx, w_ref[3 * GROUP + h * HD:3 * GROUP + (h + 1) * HD, :])
    o_ref[...] = acc


def _merge(x2, ya, yb, yc, proj, km, vm, qg, w_out, bsz, s):
    nb = s // M_BM
    t = bsz * s
    blk = lambda: pl.BlockSpec((M_BM, GROUP), lambda i: (i, 0))
    col = lambda cid: pl.BlockSpec((M_BM, GROUP), lambda i, cid=cid: (i, cid))
    return pl.pallas_call(
        _merge_body,
        grid=(t // M_BM,),
        in_specs=[pl.BlockSpec((M_BM, D_MODEL), lambda i: (i, 0)),
                  blk(), blk(), blk(), col(C_XQ), col(C_XG),
                  pl.BlockSpec((1, N_MEM, GROUP), lambda i: (i // nb, 0, 0)),
                  pl.BlockSpec((1, N_MEM, GROUP), lambda i: (i // nb, 0, 0)),
                  pl.BlockSpec((1, HD), lambda i: (0, 0)),
                  pl.BlockSpec((4 * GROUP, D_MODEL), lambda i: (0, 0))],
        out_specs=pl.BlockSpec((M_BM, D_MODEL), lambda i: (i, 0)),
        out_shape=jax.ShapeDtypeStruct((t, D_MODEL), F32),
        compiler_params=_cp(("arbitrary",), 48),
        name="xattn_out_proj",
    )(x2, ya, yb, yc, proj, proj, km, vm, qg, w_out)


def _layer(x2, mem, cosm, sinm, layer_idx, lb, norm_g, w_in, mlstm_gate_b, hgrn_norm_g,
           diff_qk_norm_g, diff_lambda, diff_subln_g, mlstm_conv_w, mlstm_conv_b,
           mlstm_norm_g, mem_norm_g, w_mem_kv, xattn_qk_norm_g, w_out, bsz, s):
    g12 = 12 * GROUP
    w_main = jnp.concatenate([w_in[:, :g12], w_in[:, g12 + 2 * HEADS:]], axis=1).astype(BF16)
    w_gate = w_in[:, g12:g12 + 2 * HEADS]
    wg = jnp.pad(w_gate, ((0, 0), (0, LANES - 2 * HEADS))).astype(BF16)
    wgt = jnp.pad(w_gate.T, ((0, 8), (0, 0))).astype(BF16)
    proj, gcol, grow = _inproj(x2, norm_g[None, :], w_main, wg, wgt)

    ya = _hgrn(proj, lb[None, :], hgrn_norm_g[None, :], bsz, s)

    gq = jnp.tile(diff_qk_norm_g[0], GROUP // B_DQK)[None, :]
    gk = jnp.tile(diff_qk_norm_g[1], GROUP // B_DQK)[None, :]
    qt, kk, vt = _bpre(proj, cosm, sinm, gq, gk, bsz, s)
    lam_init = 0.8 - 0.6 * math.exp(-0.3 * layer_idx)
    yb = _flash(qt, kk, vt, proj, diff_lambda, diff_subln_g[None, :], diff_qk_norm_g, bsz, s, lam_init)

    gbl = jnp.pad(mlstm_gate_b, (0, LANES - 2 * HEADS))[None, :]
    yc = _mlstm(proj, gcol, grow, mlstm_conv_w, mlstm_conv_b[None, :], gbl,
                mlstm_gate_b[:, None], mlstm_norm_g[None, :], bsz, s)

    km, vm = _memkv(mem, mem_norm_g[None, :], w_mem_kv.astype(BF16), xattn_qk_norm_g[1][None, :])
    return _merge(x2, ya, yb, yc, proj, km, vm, xattn_qk_norm_g[0][None, :],
                  w_out.astype(BF16), bsz, s)


def kernel(x, mem, positions, norm_g, w_in, mlstm_gate_b, hgrn_lb_logits, hgrn_norm_g,
           diff_qk_norm_g, diff_lambda, diff_subln_g, mlstm_conv_w, mlstm_conv_b,
           mlstm_norm_g, mem_norm_g, w_mem_kv, xattn_qk_norm_g, w_out):
    bsz, s, d = x.shape
    depth = norm_g.shape[0]
    assert d == D_MODEL and s % IN_BM == 0 and mem.shape[1] == N_MEM
    cosm, sinm = _rope_tables(positions)
    sm = jax.nn.softmax(hgrn_lb_logits.astype(F32), axis=0)
    lower_bounds = jnp.cumsum(sm, axis=0) - sm[0]
    x2 = x.reshape(bsz * s, d)
    for l in range(depth):
        x2 = _layer(x2, mem, cosm, sinm, l, lower_bounds[l], norm_g[l], w_in[l], mlstm_gate_b[l],
                    hgrn_norm_g[l], diff_qk_norm_g[l], diff_lambda[l], diff_subln_g[l],
                    mlstm_conv_w[l], mlstm_conv_b[l], mlstm_norm_g[l], mem_norm_g[l],
                    w_mem_kv[l], xattn_qk_norm_g[l], w_out[l], bsz, s)
    return x2.reshape(bsz, s, d)
```

```python
import functools
import math

import jax
import jax.numpy as jnp
from jax import lax
from jax.experimental import pallas as pl
from jax.experimental.pallas import tpu as pltpu

F32 = jnp.float32
BF16 = jnp.bfloat16
HI = lax.Precision.HIGHEST

D_MODEL = 1024
N_MEM = 256
GROUP = 512
HEADS = 4
HD = 128
B_DQK = 64
ROPE_DIM = 16
ROPE_THETA = 500000.0
CONV_K = 4
EPS = 1e-6
NEG = -1e30
TINY = 1e-30
LOG2E = math.log2(math.e)
MAX_UNSHIFTED_LOG2_SCORE = 100.0

LANES = 128
N_MAIN = 15 * GROUP

(C_AQ, C_AF, C_AI, C_AG, C_BQ, C_BK, C_BV, C_BG,
 C_CQ, C_CK, C_CV, C_CO, C_CG, C_XQ, C_XG) = range(15)

IN_BM, IN_BN = 1024, 1536
R_BLK = 512
A_CH, A_SUB = 64, 16
A_MAX_SUB_LOG2_DECAY = 60.0
C_CH = 128
P_BM = 512
BQ = 1024
BK = 1024
BKD = 512
M_BM = 512
T_BM = 1024

NT = (((1,), (1,)), ((), ()))
TN = (((0,), (0,)), ((), ()))


def _cp(sem, vmem_mib):
    return pltpu.CompilerParams(dimension_semantics=sem, vmem_limit_bytes=vmem_mib * 2 ** 20)


def _silu(x):
    return x * jax.nn.sigmoid(x)


def _log_sigmoid(x):
    return jnp.minimum(x, 0.0) - jnp.log1p(jnp.exp(-jnp.abs(x)))


def _rms(x, g):
    return x * lax.rsqrt(jnp.mean(x * x, axis=-1, keepdims=True) + EPS) * g


def _bdot(a, b):
    return jnp.dot(a, b, preferred_element_type=F32)


def _rope_body(pos_ref, invf_ref, cos_ref, sin_ref):
    ang = pos_ref[...].astype(F32) * invf_ref[...]
    lane = lax.broadcasted_iota(jnp.int32, ang.shape, 1) % B_DQK
    c, s = jnp.cos(ang), jnp.sin(ang)
    cos_ref[...] = jnp.where(lane < ROPE_DIM, c, 1.0)
    sin_ref[...] = jnp.where(lane < ROPE_DIM // 2, -s, jnp.where(lane < ROPE_DIM, s, 0.0))


def _rope_tables(positions):
    t = positions.size
    inv_freq = ROPE_THETA ** (-jnp.arange(0, ROPE_DIM, 2, dtype=F32) / ROPE_DIM)
    invf = jnp.tile(inv_freq, LANES // (ROPE_DIM // 2))[None, :]
    return pl.pallas_call(
        _rope_body,
        grid=(t // T_BM,),
        in_specs=[pl.BlockSpec((T_BM, 1), lambda i: (i, 0)),
                  pl.BlockSpec((1, LANES), lambda i: (0, 0))],
        out_specs=[pl.BlockSpec((T_BM, LANES), lambda i: (i, 0))] * 2,
        out_shape=[jax.ShapeDtypeStruct((t, LANES), F32)] * 2,
        compiler_params=_cp(("arbitrary",), 32),
        name="rope_tables",
    )(positions.reshape(t, 1), invf)


def _inproj_body(x_ref, g_ref, w_ref, wg_ref, wgt_ref, o_ref, gc_ref, gr_ref, h_ref):
    @pl.when(pl.program_id(1) == 0)
    def _():
        h = _rms(x_ref[...], g_ref[...]).astype(BF16)
        h_ref[...] = h
        gc_ref[...] = _bdot(h, wg_ref[...])
        gr = lax.dot_general(wgt_ref[...], h, NT, preferred_element_type=F32)
        for j in range(IN_BM // LANES):
            gr_ref[j] = gr[:8, j * LANES:(j + 1) * LANES]

    o_ref[...] = _bdot(h_ref[...], w_ref[...]).astype(o_ref.dtype)


def _inproj(x2, g, w_main, wg, wgt):
    t = x2.shape[0]
    return pl.pallas_call(
        _inproj_body,
        grid=(t // IN_BM, N_MAIN // IN_BN),
        in_specs=[pl.BlockSpec((IN_BM, D_MODEL), lambda i, j: (i, 0)),
                  pl.BlockSpec((1, D_MODEL), lambda i, j: (0, 0)),
                  pl.BlockSpec((D_MODEL, IN_BN), lambda i, j: (0, j)),
                  pl.BlockSpec((D_MODEL, LANES), lambda i, j: (0, 0)),
                  pl.BlockSpec((16, D_MODEL), lambda i, j: (0, 0))],
        out_specs=[pl.BlockSpec((IN_BM, IN_BN), lambda i, j: (i, j)),
                   pl.BlockSpec((IN_BM, LANES), lambda i, j: (i, 0)),
                   pl.BlockSpec((IN_BM // LANES, 8, LANES), lambda i, j: (i, 0, 0))],
        out_shape=[jax.ShapeDtypeStruct((t, N_MAIN), BF16),
                   jax.ShapeDtypeStruct((t, LANES), F32),
                   jax.ShapeDtypeStruct((t // LANES, 8, LANES), F32)],
        scratch_shapes=[pltpu.VMEM((IN_BM, D_MODEL), BF16)],
        compiler_params=_cp(("arbitrary", "arbitrary"), 48),
        name="in_proj",
    )(x2, g, w_main, wg, wgt)


def _hgrn_prep(rows, aq_ref, af_ref, ai_ref, ag_ref, lb, oml, tril_ref):
    nsub = A_CH // A_SUB
    fa = af_ref[rows, :].astype(F32)
    q = _silu(aq_ref[rows, :].astype(F32)) * (HD ** -0.5)
    v = ai_ref[rows, :].astype(F32)
    gate = _silu(ag_ref[rows, :].astype(F32))
    f = lb + oml * jax.nn.sigmoid(fa)
    logf = jnp.log2(jnp.maximum(f, TINY))
    ka = oml * jax.nn.sigmoid(-fa)
    b = jnp.dot(tril_ref[...], logf, precision=HI, preferred_element_type=F32)
    ends = [b[(i + 1) * A_SUB - 1:(i + 1) * A_SUB, :] for i in range(nsub)]
    spans = [-ends[0]] + [ends[i - 1] - ends[i] for i in range(1, nsub)]
    span = jnp.max(jnp.concatenate(spans, axis=0))
    return (q, ka, v, gate, b), span


def _hgrn_finish(pre, rows, ng, y_ref, st_ref, bounded):
    q, ka, v, gate, b = pre
    nsub = A_CH // A_SUB
    half = A_SUB // 2
    row = lax.broadcasted_iota(jnp.int32, (A_CH, HD), 0)
    tsub = lax.broadcasted_iota(jnp.int32, (nsub, half, HD), 1)
    ti = lax.broadcasted_iota(jnp.int32, (A_CH, A_CH), 0)
    si = lax.broadcasted_iota(jnp.int32, (A_CH, A_CH), 1)
    blast = b[A_CH - 1:A_CH, :]
    qdec = q * jnp.exp2(b)
    kdec = ka * jnp.exp2(blast - b)
    sdec = jnp.exp2(blast)
    for h in range(HEADS):
        sl = slice(h * HD, (h + 1) * HD)
        qh, kh, vh, bh = q[:, sl], ka[:, sl], v[:, sl], b[:, sl]
        vhb = vh.astype(BF16)
        st = st_ref[h]
        o = lax.dot_general(qdec[:, sl].astype(BF16), st.astype(BF16), NT, preferred_element_type=F32)
        parts = []
        for i in range(nsub):
            lo = i * A_SUB
            hi = lo + A_SUB if bounded else lo
            if hi == 0:
                parts.append(jnp.zeros((A_SUB, A_CH), F32))
                continue
            bi = bh[lo - 1:lo, :] if i else jnp.zeros((1, HD), F32)
            qi = (qh[lo:lo + A_SUB] * jnp.exp2(bh[lo:lo + A_SUB] - bi)).astype(BF16)
            kp = kh * jnp.exp2(jnp.where(row < hi, bi - bh, NEG))
            parts.append(lax.dot_general(qi, kp.astype(BF16), NT, preferred_element_type=F32))
        amat = jnp.concatenate(parts, axis=0)
        if bounded:
            amat = jnp.where(si <= ti, amat, 0.0)
        o = o + _bdot(amat.astype(BF16), vhb)
        if not bounded:
            q3 = qh.reshape(nsub, A_SUB, HD)
            k3 = kh.reshape(nsub, A_SUB, HD)
            b3 = bh.reshape(nsub, A_SUB, HD)
            v3 = vh.reshape(nsub, A_SUB, HD)
            q_lo, q_hi = q3[:, :half], q3[:, half:]
            b_lo, b_hi = b3[:, :half], b3[:, half:]
            acc_lo = jnp.zeros((nsub, half, HD), F32)
            acc_hi = jnp.zeros((nsub, half, HD), F32)
            for s in range(A_SUB):
                bs, ks, vs = b3[:, s:s + 1, :], k3[:, s:s + 1, :], v3[:, s:s + 1, :]
                if s < half:
                    e_lo = jnp.exp2(jnp.where(tsub >= s, b_lo - bs, NEG))
                    acc_lo = acc_lo + jnp.sum(q_lo * ks * e_lo, axis=-1, keepdims=True) * vs
                    e_hi = jnp.exp2(b_hi - bs)
                else:
                    e_hi = jnp.exp2(jnp.where(tsub >= s - half, b_hi - bs, NEG))
                acc_hi = acc_hi + jnp.sum(q_hi * ks * e_hi, axis=-1, keepdims=True) * vs
            o = o + jnp.concatenate([acc_lo, acc_hi], axis=1).reshape(A_CH, HD)
        st_ref[h] = st * sdec[:, sl] + lax.dot_general(
            vhb, kdec[:, sl].astype(BF16), TN, preferred_element_type=F32)
        y_ref[rows, sl] = (_rms(o, ng) * gate[:, sl]).astype(y_ref.dtype)


def _mlstm_chunk(rows, cv_ref, co_ref, cg_ref, gcol_ref, grow_ref, gbl, gbc, ng, tril_ref, triu_ref,
                 y_ref, qc_s, kc_s, ct_s, m_s):
    ri = lax.broadcasted_iota(jnp.int32, (C_CH, C_CH), 0)
    ci = lax.broadcasted_iota(jnp.int32, (C_CH, C_CH), 1)
    causal = ri >= ci
    ones_col = jnp.where(ci == 0, 1.0, 0.0)
    lane = lax.broadcasted_iota(jnp.int32, (C_CH, LANES), 1)
    srow = lax.broadcasted_iota(jnp.int32, (8, LANES), 0)
    gc = gcol_ref[rows, :] + gbl
    gc = jnp.where(lane < HEADS, gc, _log_sigmoid(gc)) * LOG2E
    bcols = jnp.dot(tril_ref[...], gc, precision=HI, preferred_element_type=F32)
    gr = grow_ref[...] + gbc
    gr = jnp.where(srow < HEADS, gr, _log_sigmoid(gr)) * LOG2E
    brows = jnp.dot(gr, triu_ref[...], precision=HI, preferred_element_type=F32)
    for h in range(HEADS):
        sl = slice(h * HD, (h + 1) * HD)
        b_col = bcols[:, HEADS + h:HEADS + h + 1]
        li_col = gc[:, h:h + 1]
        b_row = brows[HEADS + h:HEADS + h + 1, :]
        li_row = gr[h:h + 1, :]
        m = m_s[h:h + 1, 0:1]
        ct = ct_s[h]
        qh = qc_s[rows, sl].astype(BF16)
        kh = kc_s[rows, sl].astype(BF16)
        vh = cv_ref[rows, sl].astype(F32)

        dm = jnp.where(causal, b_col - b_row + li_row, NEG)
        inter = b_col + m
        m_t = jnp.maximum(jnp.max(dm, axis=1, keepdims=True), inter)
        w = jnp.exp2(dm - m_t)
        w_inter = jnp.exp2(inter - m_t)
        sc = lax.dot_general(qh, kh, NT, preferred_element_type=F32)
        smat = (sc * w).astype(BF16)
        vaug = jnp.concatenate([vh, ones_col], axis=1).astype(BF16)
        tot = _bdot(smat, vaug) + w_inter * _bdot(qh, ct.astype(BF16))
        num = tot[:, :HD]
        nq = tot[:, HD:HD + 1]
        hout = num / jnp.maximum(jnp.abs(nq), jnp.exp2(-m_t))

        b_last = b_col[C_CH - 1:C_CH, :]
        g = b_last - b_col + li_col
        m_new = jnp.maximum(b_last + m, jnp.max(g, axis=0, keepdims=True))
        wg = jnp.exp2(g - m_new)
        dec = jnp.exp2(b_last + m - m_new)
        upd = jnp.concatenate([vh * wg, ones_col * wg], axis=1).astype(BF16)
        ct_s[h] = dec * ct + lax.dot_general(kh, upd, TN, preferred_element_type=F32)
        m_s[h:h + 1, :] = jnp.broadcast_to(m_new, (1, LANES))

        hc = jax.nn.sigmoid(co_ref[rows, sl].astype(F32)) * hout
        y_ref[rows, sl] = (_rms(hc, ng) * _silu(cg_ref[rows, sl].astype(F32))).astype(y_ref.dtype)


def _rec_body(aq_ref, af_ref, ai_ref, ag_ref, cq_ref, ck_ref, cv_ref, co_ref, cg_ref, gcol_ref, grow_ref,
              lb_ref, ang_ref, cw_ref, cb_ref, gbl_ref, gbc_ref, cng_ref, tril_a_ref, tril_c_ref, triu_c_ref,
              ya_ref, yc_ref, st_s, xq_s, xk_s, qc_s, kc_s, ct_s, m_s):
    bsz = aq_ref.shape[0]

    @pl.when(pl.program_id(0) == 0)
    def _():
        st_s[...] = jnp.zeros_like(st_s)
        xq_s[:, 0:8, :] = jnp.zeros((bsz, 8, GROUP), F32)
        xk_s[:, 0:8, :] = jnp.zeros((bsz, 8, GROUP), F32)
        ct_s[...] = jnp.zeros_like(ct_s)
        m_s[...] = jnp.zeros_like(m_s)

    cw = cw_ref[...]
    cb = cb_ref[...]

    def conv(x, w, bias):
        t = w[0:1] * x
        for j in range(1, CONV_K):
            t = w[j:j + 1] * x + pltpu.roll(t, 1, 0)
        return t[8:] + bias

    for bi in range(bsz):
        xq_s[bi, 8:8 + R_BLK, :] = cq_ref[bi].astype(F32)
        xk_s[bi, 8:8 + R_BLK, :] = ck_ref[bi].astype(F32)
        qc_s[bi] = _silu(conv(xq_s[bi], cw[:, :GROUP], cb[:, :GROUP]))
        kc_s[bi] = _silu(conv(xk_s[bi], cw[:, GROUP:], cb[:, GROUP:])) * (HD ** -0.5)
        xq_s[bi, 0:8, :] = xq_s[bi, R_BLK:R_BLK + 8, :]
        xk_s[bi, 0:8, :] = xk_s[bi, R_BLK:R_BLK + 8, :]

    lb = lb_ref[...]
    oml = 1.0 - lb
    ang = ang_ref[...]
    cng = cng_ref[...]
    gbl = gbl_ref[...]
    gbc = gbc_ref[...]

    def chunk(c, carry):
        r0 = pl.multiple_of(c * C_CH, C_CH)
        a_rows = [pl.ds(pl.multiple_of(r0 + j * A_CH, A_CH), A_CH) for j in range(C_CH // A_CH)]
        pres, spans = zip(*[_hgrn_prep(rows, aq_ref.at[bi], af_ref.at[bi], ai_ref.at[bi], ag_ref.at[bi],
                                       lb, oml, tril_a_ref)
                            for bi in range(bsz) for rows in a_rows])
        bounded = functools.reduce(jnp.maximum, spans) <= A_MAX_SUB_LOG2_DECAY

        def rest(is_bounded):
            for bi in range(bsz):
                _mlstm_chunk(pl.ds(r0, C_CH), cv_ref.at[bi], co_ref.at[bi], cg_ref.at[bi], gcol_ref.at[bi],
                             grow_ref.at[bi, c], gbl, gbc, cng, tril_c_ref, triu_c_ref, yc_ref.at[bi],
                             qc_s.at[bi], kc_s.at[bi], ct_s.at[bi], m_s.at[bi])
            for i, pre in enumerate(pres):
                bi, j = divmod(i, len(a_rows))
                _hgrn_finish(pre, a_rows[j], ang, ya_ref.at[bi], st_s.at[bi], is_bounded)

        pl.when(bounded)(functools.partial(rest, True))
        pl.when(jnp.logical_not(bounded))(functools.partial(rest, False))
        return carry

    lax.fori_loop(0, R_BLK // C_CH, chunk, 0)


def _recurrent(proj, gcol, grow, lb, ang, cw, cb, gbl, gbc, cng, bsz, s):
    proj3 = proj.reshape(bsz, s, N_MAIN)
    col = lambda cid: pl.BlockSpec((bsz, R_BLK, GROUP), lambda t, cid=cid: (0, t, cid))
    full = lambda shp: pl.BlockSpec(shp, lambda t: (0,) * len(shp))
    tril_a = jnp.tril(jnp.ones((A_CH, A_CH), F32))
    tril_c = jnp.tril(jnp.ones((C_CH, C_CH), F32))
    out = pl.BlockSpec((bsz, R_BLK, GROUP), lambda t: (0, t, 0))
    ya, yc = pl.pallas_call(
        _rec_body,
        grid=(s // R_BLK,),
        in_specs=[col(C_AQ), col(C_AF), col(C_AI), col(C_AG),
                  col(C_CQ), col(C_CK), col(C_CV), col(C_CO), col(C_CG),
                  pl.BlockSpec((bsz, R_BLK, LANES), lambda t: (0, t, 0)),
                  pl.BlockSpec((bsz, R_BLK // LANES, 8, LANES), lambda t: (0, t, 0, 0)),
                  full((1, GROUP)), full((1, HD)),
                  full((CONV_K, 2 * GROUP)), full((1, 2 * GROUP)),
                  full((1, LANES)), full((8, 1)), full((1, HD)),
                  full((A_CH, A_CH)), full((C_CH, C_CH)), full((C_CH, C_CH))],
        out_specs=[out, out],
        out_shape=[jax.ShapeDtypeStruct((bsz, s, GROUP), BF16)] * 2,
        scratch_shapes=[pltpu.VMEM((bsz, HEADS, HD, HD), F32),
                        pltpu.VMEM((bsz, R_BLK + 8, GROUP), F32), pltpu.VMEM((bsz, R_BLK + 8, GROUP), F32),
                        pltpu.VMEM((bsz, R_BLK, GROUP), F32), pltpu.VMEM((bsz, R_BLK, GROUP), F32),
                        pltpu.VMEM((bsz, HEADS, HD, 2 * HD), F32), pltpu.VMEM((bsz, 8, LANES), F32)],
        compiler_params=_cp(("arbitrary",), 56),
        name="hgrn2_mlstm",
    )(*([proj3] * 9), gcol.reshape(bsz, s, LANES), grow.reshape(bsz, s // LANES, 8, LANES),
      lb, ang, cw, cb, gbl, gbc, cng, tril_a, tril_c, tril_c.T)
    return ya.reshape(bsz * s, GROUP), yc.reshape(bsz * s, GROUP)


def _bpre_body(bq_ref, bk_ref, bv_ref, cos_ref, sin_ref, gq_ref, gk_ref, bd_ref,
               qt_ref, k_ref, vt_ref):
    reps = GROUP // LANES
    cosm = jnp.concatenate([cos_ref[...]] * reps, axis=1)
    sinm = jnp.concatenate([sin_ref[...]] * reps, axis=1)
    lane = lax.broadcasted_iota(jnp.int32, (P_BM, GROUP), 1)
    first_half = (lane % ROPE_DIM) < ROPE_DIM // 2
    bd = bd_ref[...]

    def prep(x, g):
        ms = _bdot((x * x).astype(BF16), bd)
        xn = x * lax.rsqrt(ms + EPS) * g
        partner = jnp.where(first_half,
                            pltpu.roll(xn, GROUP - ROPE_DIM // 2, 1),
                            pltpu.roll(xn, ROPE_DIM // 2, 1))
        return xn * cosm + partner * sinm

    q = prep(bq_ref[...].astype(F32), gq_ref[...]) * (B_DQK ** -0.5 * LOG2E)
    k = prep(bk_ref[...].astype(F32), gk_ref[...])
    qt_ref[0] = q.T.astype(BF16)
    k_ref[...] = k.astype(BF16)
    vt_ref[0, 0] = bv_ref[...].astype(F32).T.astype(BF16)


def _bpre(proj, cosm, sinm, gq, gk, bsz, s):
    nb = s // P_BM
    ppk = BK // P_BM
    t = bsz * s
    col = lambda cid: pl.BlockSpec((P_BM, GROUP), lambda i, cid=cid: (i, cid))
    seg = jnp.arange(GROUP) // B_DQK
    bd = (jnp.where(seg[:, None] == seg[None, :], 1.0 / B_DQK, 0.0)).astype(BF16)
    return pl.pallas_call(
        _bpre_body,
        grid=(t // P_BM,),
        in_specs=[col(C_BQ), col(C_BK), col(C_BV),
                  pl.BlockSpec((P_BM, LANES), lambda i: (i, 0)),
                  pl.BlockSpec((P_BM, LANES), lambda i: (i, 0)),
                  pl.BlockSpec((1, GROUP), lambda i: (0, 0)),
                  pl.BlockSpec((1, GROUP), lambda i: (0, 0)),
                  pl.BlockSpec((GROUP, GROUP), lambda i: (0, 0))],
        out_specs=[pl.BlockSpec((1, GROUP, P_BM), lambda i: (i // nb, 0, i % nb)),
                   pl.BlockSpec((P_BM, GROUP), lambda i: (i, 0)),
                   pl.BlockSpec((1, 1, GROUP, P_BM),
                                lambda i: (i // nb, (i % nb) // ppk, 0, (i % nb) % ppk))],
        out_shape=[jax.ShapeDtypeStruct((bsz, GROUP, s), BF16),
                   jax.ShapeDtypeStruct((t, GROUP), BF16),
                   jax.ShapeDtypeStruct((bsz, s // BK, GROUP, BK), BF16)],
        compiler_params=_cp(("arbitrary",), 48),
        name="diff_prep",
    )(proj, proj, proj, cosm, sinm, gq, gk, bd)


def _flash_body(qt_ref, k_ref, vt_ref, bg_ref, lam_ref, sg_ref, y_ref,
                qz_s, m_s, l_s, acc_s, *, lam_init, online_max):
    qi = pl.program_id(2)
    qt = qt_ref[0]
    rowi = lax.broadcasted_iota(jnp.int32, (HD, BQ), 0)
    zero = jnp.zeros_like(qt)
    qz_s[0] = jnp.where(rowi < B_DQK, qt, zero)
    qz_s[1] = jnp.where(rowi >= B_DQK, qt, zero)
    m_s[...] = jnp.full_like(m_s, NEG)
    l_s[...] = jnp.zeros_like(l_s)
    acc_s[...] = jnp.zeros_like(acc_s)

    def step(kblk, vblk, q0, mask):
        qs = slice(q0, BQ)
        for c in range(2):
            s = _bdot(kblk, qz_s[c, :, qs])
            if mask is not None:
                s = jnp.where(mask, s, NEG)
            if online_max:
                m_old = m_s[c, :, qs]
                m_new = jnp.maximum(m_old, jnp.max(s, axis=0, keepdims=True))
                alpha = jnp.exp2(m_old - m_new)
                p = jnp.exp2(s - m_new)
                l_s[c, :, qs] = alpha * l_s[c, :, qs] + jnp.sum(p, axis=0, keepdims=True)
                acc_s[c, :, qs] = alpha * acc_s[c, :, qs] + _bdot(vblk, p.astype(BF16))
                m_s[c, :, qs] = m_new
            else:
                p = jnp.exp2(s)
                l_s[c, :, qs] = l_s[c, :, qs] + jnp.sum(p, axis=0, keepdims=True)
                acc_s[c, :, qs] = acc_s[c, :, qs] + _bdot(vblk, p.astype(BF16))

    def body(kb, carry):
        r0 = pl.multiple_of(kb * BK, BK)
        step(k_ref[pl.ds(r0, BK), :], vt_ref[0, kb], 0, None)
        return carry

    lax.fori_loop(0, qi, body, 0)
    r0 = pl.multiple_of(qi * BK, BK)
    vdiag = vt_ref[0, qi]
    for d in range(BK // BKD):
        q0 = d * BKD
        kidx = lax.broadcasted_iota(jnp.int32, (BKD, BQ - q0), 0)
        qidx = lax.broadcasted_iota(jnp.int32, (BKD, BQ - q0), 1)
        rows = pl.ds(pl.multiple_of(r0 + q0, BKD), BKD)
        step(k_ref[rows, :], vdiag[:, q0:q0 + BKD], q0, kidx <= qidx)

    lp = lam_ref[...]
    lam = (jnp.exp(jnp.sum(lp[0:1] * lp[1:2], axis=1, keepdims=True))
           - jnp.exp(jnp.sum(lp[2:3] * lp[3:4], axis=1, keepdims=True)) + lam_init)
    ot = acc_s[0] / l_s[0] - lam * (acc_s[1] / l_s[1])
    o = ot.T
    y = _rms(o, sg_ref[...]) * (1.0 - lam_init) * _silu(bg_ref[...].astype(F32))
    y_ref[...] = y.astype(y_ref.dtype)


def _flash_call(qt, kk, vt, proj, lam_p, sg, *, bsz, s, lam_init, online_max):
    nq = s // BQ
    return pl.pallas_call(
        functools.partial(_flash_body, lam_init=lam_init, online_max=online_max),
        grid=(bsz, HEADS, nq),
        in_specs=[pl.BlockSpec((1, HD, BQ), lambda b, h, q: (b, h, q)),
                  pl.BlockSpec((s, HD), lambda b, h, q: (b, h)),
                  pl.BlockSpec((1, s // BK, HD, BK), lambda b, h, q: (b, 0, h, 0)),
                  pl.BlockSpec((BQ, HD), lambda b, h, q: (b * nq + q, C_BG * HEADS + h)),
                  pl.BlockSpec((4, B_DQK), lambda b, h, q: (0, 0)),
                  pl.BlockSpec((1, HD), lambda b, h, q: (0, 0))],
        out_specs=pl.BlockSpec((BQ, HD), lambda b, h, q: (b * nq + q, h)),
        out_shape=jax.ShapeDtypeStruct((bsz * s, GROUP), BF16),
        scratch_shapes=[pltpu.VMEM((2, HD, BQ), BF16), pltpu.VMEM((2, 1, BQ), F32),
                        pltpu.VMEM((2, 1, BQ), F32), pltpu.VMEM((2, HD, BQ), F32)],
        compiler_params=_cp(("arbitrary", "arbitrary", "arbitrary"), 48),
        name="diff_flash_online" if online_max else "diff_flash",
    )(qt, kk, vt, proj, lam_p, sg)


def _flash(qt, kk, vt, proj, lam_p, sg, qk_g, bsz, s, lam_init):
    bound = (B_DQK ** 0.5) * LOG2E * jnp.max(jnp.abs(qk_g[0])) * jnp.max(jnp.abs(qk_g[1]))
    args = (qt, kk, vt, proj, lam_p, sg)
    call = functools.partial(_flash_call, bsz=bsz, s=s, lam_init=lam_init)
    return lax.cond(bound <= MAX_UNSHIFTED_LOG2_SCORE - math.log2(s),
                    functools.partial(call, online_max=False),
                    functools.partial(call, online_max=True), *args)


def _memkv_body(mem_ref, g_ref, w_ref, kg_ref, km_ref, vm_ref):
    mn = _rms(mem_ref[0], g_ref[...]).astype(BF16)
    kv = _bdot(mn, w_ref[...])
    for h in range(HEADS):
        sl = slice(h * HD, (h + 1) * HD)
        km_ref[0, :, sl] = _rms(kv[:, sl], kg_ref[...]).astype(BF16)
    vm_ref[0] = kv[:, GROUP:].astype(BF16)


def _memkv(mem, g, w, kg):
    bsz = mem.shape[0]
    return pl.pallas_call(
        _memkv_body,
        grid=(bsz,),
        in_specs=[pl.BlockSpec((1, N_MEM, D_MODEL), lambda b: (b, 0, 0)),
                  pl.BlockSpec((1, D_MODEL), lambda b: (0, 0)),
                  pl.BlockSpec((D_MODEL, 2 * GROUP), lambda b: (0, 0)),
                  pl.BlockSpec((1, HD), lambda b: (0, 0))],
        out_specs=[pl.BlockSpec((1, N_MEM, GROUP), lambda b: (b, 0, 0))] * 2,
        out_shape=[jax.ShapeDtypeStruct((bsz, N_MEM, GROUP), BF16)] * 2,
        compiler_params=_cp(("arbitrary",), 32),
        name="mem_kv",
    )(mem, g, w, kg)


def _merge_body(x_ref, ya_ref, yb_ref, yc_ref, xq_ref, xg_ref, km_ref, vm_ref, qg_ref, w_ref, o_ref):
    acc = x_ref[...]
    acc = acc + _bdot(ya_ref[...], w_ref[0:GROUP, :])
    acc = acc + _bdot(yb_ref[...], w_ref[GROUP:2 * GROUP, :])
    acc = acc + _bdot(yc_ref[...], w_ref[2 * GROUP:3 * GROUP, :])
    qg = qg_ref[...]
    yx = []
    for h in range(HEADS):
        sl = slice(h * HD, (h + 1) * HD)
        qn = (_rms(xq_ref[:, sl].astype(F32), qg) * (HD ** -0.5)).astype(BF16)
        s = lax.dot_general(qn, km_ref[0, :, sl], NT, preferred_element_type=F32)
        p = jnp.exp(s - jnp.max(s, axis=-1, keepdims=True))
        l = jnp.sum(p, axis=-1, keepdims=True)
        ox = _bdot(p.astype(BF16), vm_ref[0, :, sl]) / l
        yx.append((ox * _silu(xg_ref[:, sl].astype(F32))).astype(BF16))
    o_ref[...] = acc + _bdot(jnp.concatenate(yx, axis=1), w_ref[3 * GROUP:4 * GROUP, :])


def _merge(x2, ya, yb, yc, proj, km, vm, qg, w_out, bsz, s):
    nb = s // M_BM
    t = bsz * s
    blk = lambda: pl.BlockSpec((M_BM, GROUP), lambda i: (i, 0))
    col = lambda cid: pl.BlockSpec((M_BM, GROUP), lambda i, cid=cid: (i, cid))
    return pl.pallas_call(
        _merge_body,
        grid=(t // M_BM,),
        in_specs=[pl.BlockSpec((M_BM, D_MODEL), lambda i: (i, 0)),
                  blk(), blk(), blk(), col(C_XQ), col(C_XG),
                  pl.BlockSpec((1, N_MEM, GROUP), lambda i: (i // nb, 0, 0)),
                  pl.BlockSpec((1, N_MEM, GROUP), lambda i: (i // nb, 0, 0)),
                  pl.BlockSpec((1, HD), lambda i: (0, 0)),
                  pl.BlockSpec((4 * GROUP, D_MODEL), lambda i: (0, 0))],
        out_specs=pl.BlockSpec((M_BM, D_MODEL), lambda i: (i, 0)),
        out_shape=jax.ShapeDtypeStruct((t, D_MODEL), F32),
        compiler_params=_cp(("arbitrary",), 48),
        name="xattn_out_proj",
    )(x2, ya, yb, yc, proj, proj, km, vm, qg, w_out)


def _layer(x2, mem, cosm, sinm, layer_idx, lb, norm_g, w_in, mlstm_gate_b, hgrn_norm_g,
           diff_qk_norm_g, diff_lambda, diff_subln_g, mlstm_conv_w, mlstm_conv_b,
           mlstm_norm_g, mem_norm_g, w_mem_kv, xattn_qk_norm_g, w_out, bsz, s):
    g12 = 12 * GROUP
    w_main = jnp.concatenate([w_in[:, :g12], w_in[:, g12 + 2 * HEADS:]], axis=1).astype(BF16)
    w_gate = w_in[:, g12:g12 + 2 * HEADS]
    wg = jnp.pad(w_gate, ((0, 0), (0, LANES - 2 * HEADS))).astype(BF16)
    wgt = jnp.pad(w_gate.T, ((0, 8), (0, 0))).astype(BF16)
    proj, gcol, grow = _inproj(x2, norm_g[None, :], w_main, wg, wgt)

    gbl = jnp.pad(mlstm_gate_b, (0, LANES - 2 * HEADS))[None, :]
    ya, yc = _recurrent(proj, gcol, grow, lb[None, :], hgrn_norm_g[None, :], mlstm_conv_w,
                        mlstm_conv_b[None, :], gbl, mlstm_gate_b[:, None], mlstm_norm_g[None, :], bsz, s)

    gq = jnp.tile(diff_qk_norm_g[0], GROUP // B_DQK)[None, :]
    gk = jnp.tile(diff_qk_norm_g[1], GROUP // B_DQK)[None, :]
    qt, kk, vt = _bpre(proj, cosm, sinm, gq, gk, bsz, s)
    lam_init = 0.8 - 0.6 * math.exp(-0.3 * layer_idx)
    yb = _flash(qt, kk, vt, proj, diff_lambda, diff_subln_g[None, :], diff_qk_norm_g, bsz, s, lam_init)

    km, vm = _memkv(mem, mem_norm_g[None, :], w_mem_kv.astype(BF16), xattn_qk_norm_g[1][None, :])
    return _merge(x2, ya, yb, yc, proj, km, vm, xattn_qk_norm_g[0][None, :],
                  w_out.astype(BF16), bsz, s)


def kernel(x, mem, positions, norm_g, w_in, mlstm_gate_b, hgrn_lb_logits, hgrn_norm_g,
           diff_qk_norm_g, diff_lambda, diff_subln_g, mlstm_conv_w, mlstm_conv_b,
           mlstm_norm_g, mem_norm_g, w_mem_kv, xattn_qk_norm_g, w_out):
    bsz, s, d = x.shape
    depth = norm_g.shape[0]
    assert d == D_MODEL and s % IN_BM == 0 and mem.shape[1] == N_MEM
    cosm, sinm = _rope_tables(positions)
    sm = jax.nn.softmax(hgrn_lb_logits.astype(F32), axis=0)
    lower_bounds = jnp.cumsum(sm, axis=0) - sm[0]
    x2 = x.reshape(bsz * s, d)
    for l in range(depth):
        x2 = _layer(x2, mem, cosm, sinm, l, lower_bounds[l], norm_g[l], w_in[l], mlstm_gate_b[l],
                    hgrn_norm_g[l], diff_qk_norm_g[l], diff_lambda[l], diff_subln_g[l],
                    mlstm_conv_w[l], mlstm_conv_b[l], mlstm_norm_g[l], mem_norm_g[l],
                    w_mem_kv[l], xattn_qk_norm_g[l], w_out[l], bsz, s)
    return x2.reshape(bsz, s, d)
```

```python
import functools
import math

import jax
import jax.numpy as jnp
from jax import lax
from jax.experimental import pallas as pl
from jax.experimental.pallas import tpu as pltpu

F32 = jnp.float32
BF16 = jnp.bfloat16
HI = lax.Precision.HIGHEST

D_MODEL = 1024
N_MEM = 256
GROUP = 512
HEADS = 4
HD = 128
B_DQK = 64
ROPE_DIM = 16
ROPE_THETA = 500000.0
CONV_K = 4
EPS = 1e-6
NEG = -1e30
TINY = 1e-30
LOG2E = math.log2(math.e)
MAX_UNSHIFTED_LOG2_SCORE = 100.0

LANES = 128
N_MAIN = 15 * GROUP

(C_AQ, C_AF, C_AI, C_AG, C_BQ, C_BK, C_BV, C_BG,
 C_CQ, C_CK, C_CV, C_CO, C_CG, C_XQ, C_XG) = range(15)

IN_BM, IN_BN = 1024, 1536
R_BLK = 512
A_CH, A_SUB = 128, 16
A_MAX_SUB_LOG2_DECAY = 60.0
C_CH = 128
P_BM = 512
BQ = 1024
BK = 1024
BKD = 512
M_BM = 512
T_BM = 1024

NT = (((1,), (1,)), ((), ()))
TN = (((0,), (0,)), ((), ()))


def _cp(sem, vmem_mib):
    return pltpu.CompilerParams(dimension_semantics=sem, vmem_limit_bytes=vmem_mib * 2 ** 20)


def _silu(x):
    return x * jax.nn.sigmoid(x)


def _log_sigmoid(x):
    return jnp.minimum(x, 0.0) - jnp.log1p(jnp.exp(-jnp.abs(x)))


def _rms(x, g):
    return x * lax.rsqrt(jnp.mean(x * x, axis=-1, keepdims=True) + EPS) * g


def _bdot(a, b):
    return jnp.dot(a, b, preferred_element_type=F32)


def _unit_rms_heads(xs, mean_ref):
    n = xs[0].shape[0]
    x = jnp.concatenate(xs, axis=0)
    y = x * lax.rsqrt(_bdot((x * x).astype(BF16), mean_ref[...]) + EPS)
    return [y[i * n:(i + 1) * n] for i in range(len(xs))]


def _cumsum_rows(x):
    row = lax.broadcasted_iota(jnp.int32, x.shape, 0)
    d = 1
    while d < x.shape[0]:
        x = x + jnp.where(row >= d, pltpu.roll(x, d, 0), 0.0)
        d *= 2
    return x


def _colmat(r):
    return jnp.broadcast_to(r, (r.shape[1], r.shape[1])).T


def _rope_body(pos_ref, invf_ref, cos_ref, sin_ref):
    ang = pos_ref[...].astype(F32) * invf_ref[...]
    lane = lax.broadcasted_iota(jnp.int32, ang.shape, 1) % B_DQK
    c, s = jnp.cos(ang), jnp.sin(ang)
    cos_ref[...] = jnp.where(lane < ROPE_DIM, c, 1.0)
    sin_ref[...] = jnp.where(lane < ROPE_DIM // 2, -s, jnp.where(lane < ROPE_DIM, s, 0.0))


def _rope_tables(positions):
    t = positions.size
    inv_freq = ROPE_THETA ** (-jnp.arange(0, ROPE_DIM, 2, dtype=F32) / ROPE_DIM)
    invf = jnp.tile(inv_freq, LANES // (ROPE_DIM // 2))[None, :]
    return pl.pallas_call(
        _rope_body,
        grid=(t // T_BM,),
        in_specs=[pl.BlockSpec((T_BM, 1), lambda i: (i, 0)),
                  pl.BlockSpec((1, LANES), lambda i: (0, 0))],
        out_specs=[pl.BlockSpec((T_BM, LANES), lambda i: (i, 0))] * 2,
        out_shape=[jax.ShapeDtypeStruct((t, LANES), F32)] * 2,
        compiler_params=_cp(("arbitrary",), 32),
        name="rope_tables",
    )(positions.reshape(t, 1), invf)


def _inproj_body(x_ref, g_ref, w_ref, wgt_ref, o_ref, gr_ref, h_ref):
    @pl.when(pl.program_id(1) == 0)
    def _():
        h = _rms(x_ref[...], g_ref[...]).astype(BF16)
        h_ref[...] = h
        gr = lax.dot_general(wgt_ref[...], h, NT, preferred_element_type=F32)
        for j in range(IN_BM // LANES):
            gr_ref[j] = gr[:8, j * LANES:(j + 1) * LANES]

    o_ref[...] = _bdot(h_ref[...], w_ref[...]).astype(o_ref.dtype)


def _inproj(x2, g, w_main, wgt):
    t = x2.shape[0]
    return pl.pallas_call(
        _inproj_body,
        grid=(t // IN_BM, N_MAIN // IN_BN),
        in_specs=[pl.BlockSpec((IN_BM, D_MODEL), lambda i, j: (i, 0)),
                  pl.BlockSpec((1, D_MODEL), lambda i, j: (0, 0)),
                  pl.BlockSpec((D_MODEL, IN_BN), lambda i, j: (0, j)),
                  pl.BlockSpec((16, D_MODEL), lambda i, j: (0, 0))],
        out_specs=[pl.BlockSpec((IN_BM, IN_BN), lambda i, j: (i, j)),
                   pl.BlockSpec((IN_BM // LANES, 8, LANES), lambda i, j: (i, 0, 0))],
        out_shape=[jax.ShapeDtypeStruct((t, N_MAIN), BF16),
                   jax.ShapeDtypeStruct((t // LANES, 8, LANES), F32)],
        scratch_shapes=[pltpu.VMEM((IN_BM, D_MODEL), BF16)],
        compiler_params=_cp(("arbitrary", "arbitrary"), 48),
        name="in_proj",
    )(x2, g, w_main, wgt)


def _hgrn_gates(rows, af_ref, lb, oml):
    sg = jax.nn.sigmoid(af_ref[rows, :].astype(F32))
    return jnp.log2(jnp.maximum(lb + oml * sg, TINY)), oml * (1.0 - sg)


def _hgrn_max_sub_decay(rows, af_ref, lb, oml):
    logf, _ = _hgrn_gates(rows, af_ref, lb, oml)
    sums = jnp.sum(logf.reshape(A_CH // A_SUB, A_SUB, GROUP), axis=1)
    return -jnp.min(sums)


def _hgrn_prep(rows, aq_ref, af_ref, ai_ref, ag_ref, lb, oml):
    q = _silu(aq_ref[rows, :].astype(F32)) * (HD ** -0.5)
    v = ai_ref[rows, :].astype(F32)
    gate = _silu(ag_ref[rows, :].astype(F32))
    logf, ka = _hgrn_gates(rows, af_ref, lb, oml)
    return q, ka, v, gate, _cumsum_rows(logf)


def _hgrn_finish(pre, st_ref, vt_ref, bounded):
    q, ka, v, gate, b = pre
    nsub = A_CH // A_SUB
    half = A_SUB // 2
    row = lax.broadcasted_iota(jnp.int32, (A_CH, HD), 0)
    tsub = lax.broadcasted_iota(jnp.int32, (nsub, half, HD), 1)
    ti = lax.broadcasted_iota(jnp.int32, (A_CH, A_CH), 0)
    si = lax.broadcasted_iota(jnp.int32, (A_CH, A_CH), 1)
    blast = b[A_CH - 1:A_CH, :]
    qdec = q * jnp.exp2(b)
    kdec = ka * jnp.exp2(blast - b)
    sdec = jnp.exp2(blast)
    outs = []
    for h in range(HEADS):
        sl = slice(h * HD, (h + 1) * HD)
        qh, kh, vh, bh = q[:, sl], ka[:, sl], v[:, sl], b[:, sl]
        vhb = vh.astype(BF16)
        st = st_ref[h]
        o = lax.dot_general(qdec[:, sl].astype(BF16), st.astype(BF16), NT, preferred_element_type=F32)
        parts = []
        for i in range(nsub):
            lo = i * A_SUB
            hi = lo + A_SUB if bounded else lo
            if hi == 0:
                parts.append(jnp.zeros((A_SUB, A_CH), F32))
                continue
            bi = bh[lo - 1:lo, :] if i else jnp.zeros((1, HD), F32)
            qi = (qh[lo:lo + A_SUB] * jnp.exp2(bh[lo:lo + A_SUB] - bi)).astype(BF16)
            kp = (kh[:hi] * jnp.exp2(bi - bh[:hi])).astype(BF16)
            part = lax.dot_general(qi, kp, NT, preferred_element_type=F32)
            if hi < A_CH:
                part = jnp.concatenate([part, jnp.zeros((A_SUB, A_CH - hi), F32)], axis=1)
            parts.append(part)
        amat = jnp.concatenate(parts, axis=0)
        if bounded:
            amat = jnp.where(si <= ti, amat, 0.0)
        o = o + _bdot(amat.astype(BF16), vhb)
        if not bounded:
            q3 = qh.reshape(nsub, A_SUB, HD)
            k3 = kh.reshape(nsub, A_SUB, HD)
            b3 = bh.reshape(nsub, A_SUB, HD)
            v3 = vh.reshape(nsub, A_SUB, HD)
            q_lo, q_hi = q3[:, :half], q3[:, half:]
            b_lo, b_hi = b3[:, :half], b3[:, half:]
            acc_lo = jnp.zeros((nsub, half, HD), F32)
            acc_hi = jnp.zeros((nsub, half, HD), F32)
            for s in range(A_SUB):
                bs, ks, vs = b3[:, s:s + 1, :], k3[:, s:s + 1, :], v3[:, s:s + 1, :]
                if s < half:
                    e_lo = jnp.exp2(jnp.where(tsub >= s, b_lo - bs, NEG))
                    acc_lo = acc_lo + jnp.sum(q_lo * ks * e_lo, axis=-1, keepdims=True) * vs
                    e_hi = jnp.exp2(b_hi - bs)
                else:
                    e_hi = jnp.exp2(jnp.where(tsub >= s - half, b_hi - bs, NEG))
                acc_hi = acc_hi + jnp.sum(q_hi * ks * e_hi, axis=-1, keepdims=True) * vs
            o = o + jnp.concatenate([acc_lo, acc_hi], axis=1).reshape(A_CH, HD)
        st_ref[h] = st * sdec[:, sl] + _bdot(vt_ref[sl, :].astype(BF16), kdec[:, sl].astype(BF16))
        outs.append(o)
    return outs


def _mlstm_gate_rows(grow_ref, gbc, triu_ref):
    nc = grow_ref.shape[0]
    g = grow_ref[...] + gbc
    srow = lax.broadcasted_iota(jnp.int32, g.shape, 1)
    g = jnp.where(srow < HEADS, g, _log_sigmoid(g)) * LOG2E
    cum = jnp.dot(g.reshape(nc * 8, LANES), triu_ref[...], precision=HI,
                  preferred_element_type=F32).reshape(nc, 8, LANES)
    b = cum[:, HEADS:, :]
    ab = jnp.concatenate([g[:, :HEADS, :] - b, b], axis=1)
    cm = ab.reshape(nc * 8, LANES)
    lane = lax.broadcasted_iota(jnp.int32, cm.shape, 1)
    d = 1
    while d < LANES:
        cm = jnp.maximum(cm, jnp.where(lane >= d, pltpu.roll(cm, d, 1), NEG))
        d *= 2
    return ab, cm.reshape(nc, 8, LANES)


def _mlstm_vectors(ab_ref, cm_ref, m_s):
    ab = ab_ref[...]
    a4, b4 = ab[:HEADS], ab[HEADS:]
    m_prev = m_s[0:HEADS, :]
    mx4 = jnp.maximum(cm_ref[0:HEADS, :], m_prev)
    mt4 = b4 + mx4
    mx_last = jnp.broadcast_to(mx4[:, C_CH - 1:C_CH], (HEADS, LANES))
    wg4 = jnp.exp2(a4 - mx_last)
    dec4 = jnp.exp2(m_prev - mx_last)
    m_s[0:HEADS, :] = jnp.broadcast_to(mt4[:, C_CH - 1:C_CH], (HEADS, LANES))
    cols = [(_colmat(mx4[h:h + 1]), _colmat(mt4[h:h + 1])) for h in range(HEADS)]
    return a4, wg4, dec4, m_prev, cols


def _mlstm_chunk(rows, per_batch, vecs):
    ri = lax.broadcasted_iota(jnp.int32, (C_CH, C_CH), 0)
    ci = lax.broadcasted_iota(jnp.int32, (C_CH, C_CH), 1)
    causal = ri >= ci
    ones_blk = jnp.ones((C_CH, HD), BF16)
    hcs = []
    for (cv_ref, co_ref, qc_s, kt_ref, ct_s), (a4, wg4, dec4, m_prev, cols) in zip(per_batch, vecs):
        for h in range(HEADS):
            sl = slice(h * HD, (h + 1) * HD)
            row = slice(h, h + 1)
            mxm, mtm = cols[h]
            ct = ct_s[h]
            qf = qc_s[rows, sl]
            kt = kt_ref[sl, :]
            vaug = jnp.concatenate([cv_ref[rows, sl], ones_blk], axis=1)

            w = jnp.exp2(jnp.where(causal, a4[row] - mxm, NEG))
            sc = _bdot(qf.astype(BF16), kt.astype(BF16))
            lhs = jnp.concatenate([(sc * w).astype(BF16),
                                   (qf * jnp.exp2(m_prev[row] - mxm)).astype(BF16)], axis=1)
            tot = _bdot(lhs, jnp.concatenate([vaug, ct.astype(BF16)], axis=0))
            hout = tot[:, :HD] / jnp.maximum(jnp.abs(tot[:, HD:]), jnp.exp2(-mtm))

            dec2 = jnp.concatenate([dec4[row], dec4[row]], axis=1)
            ct_s[h] = dec2 * ct + _bdot((kt * wg4[row]).astype(BF16), vaug)
            hcs.append(jax.nn.sigmoid(co_ref[rows, sl].astype(F32)) * hout)
    return hcs


def _rec_body(aq_ref, af_ref, ai_ref, ag_ref, cq_ref, ck_ref, cv_ref, co_ref, cg_ref, grow_ref,
              lb_ref, ang_ref, cw_ref, cb_ref, gbc_ref, cng_ref, triu_ref, mean_ref,
              ya_ref, yc_ref, st_s, xq_s, xk_s, qc_s, kt_s, vt_s, ab_s, cm_s, ct_s, m_s):
    bsz = aq_ref.shape[0]
    nchunk = R_BLK // C_CH

    @pl.when(pl.program_id(0) == 0)
    def _():
        st_s[...] = jnp.zeros_like(st_s)
        xq_s[:, 0:8, :] = jnp.zeros((bsz, 8, GROUP), F32)
        xk_s[:, 0:8, :] = jnp.zeros((bsz, 8, GROUP), F32)
        ct_s[...] = jnp.zeros_like(ct_s)
        m_s[...] = jnp.zeros_like(m_s)

    cw = cw_ref[...]
    cb = cb_ref[...]
    gbc = gbc_ref[...]

    def conv(x, w, bias):
        t = w[0:1] * x
        for j in range(1, CONV_K):
            t = w[j:j + 1] * x + pltpu.roll(t, 1, 0)
        return t[8:] + bias

    for bi in range(bsz):
        xq_s[bi, 8:8 + R_BLK, :] = cq_ref[bi].astype(F32)
        xk_s[bi, 8:8 + R_BLK, :] = ck_ref[bi].astype(F32)
        qc_s[bi] = _silu(conv(xq_s[bi], cw[:, :GROUP], cb[:, :GROUP]))
        kc = _silu(conv(xk_s[bi], cw[:, GROUP:], cb[:, GROUP:])) * (HD ** -0.5)
        for c in range(nchunk):
            kt_s[bi, c] = kc[c * C_CH:(c + 1) * C_CH, :].T
            vt_s[bi, c] = ai_ref[bi, c * C_CH:(c + 1) * C_CH, :].astype(F32).T
        xq_s[bi, 0:8, :] = xq_s[bi, R_BLK:R_BLK + 8, :]
        xk_s[bi, 0:8, :] = xk_s[bi, R_BLK:R_BLK + 8, :]
        ab_s[bi], cm_s[bi] = _mlstm_gate_rows(grow_ref.at[bi], gbc, triu_ref)

    lb = lb_ref[...]
    oml = 1.0 - lb
    ang = ang_ref[...]
    cng = cng_ref[...]

    def chunk(c, carry):
        rows = pl.ds(pl.multiple_of(c * C_CH, C_CH), C_CH)
        spans = [_hgrn_max_sub_decay(rows, af_ref.at[bi], lb, oml) for bi in range(bsz)]
        bounded = functools.reduce(jnp.maximum, spans) <= A_MAX_SUB_LOG2_DECAY

        def rest(is_bounded):
            vecs = [_mlstm_vectors(ab_s.at[bi, c], cm_s.at[bi, c], m_s.at[bi]) for bi in range(bsz)]
            pres = [_hgrn_prep(rows, aq_ref.at[bi], af_ref.at[bi], ai_ref.at[bi], ag_ref.at[bi], lb, oml)
                    for bi in range(bsz)]
            outs = [o for bi in range(bsz)
                    for o in _hgrn_finish(pres[bi], st_s.at[bi], vt_s.at[bi, c], is_bounded)]
            outs = _unit_rms_heads(outs, mean_ref)
            per_batch = [(cv_ref.at[bi], co_ref.at[bi], qc_s.at[bi], kt_s.at[bi, c], ct_s.at[bi])
                         for bi in range(bsz)]
            hcs = _unit_rms_heads(_mlstm_chunk(rows, per_batch, vecs), mean_ref)
            for i in range(bsz * HEADS):
                bi, h = divmod(i, HEADS)
                sl = slice(h * HD, (h + 1) * HD)
                ya_ref[bi, rows, sl] = (outs[i] * ang * pres[bi][3][:, sl]).astype(ya_ref.dtype)
                yc_ref[bi, rows, sl] = (hcs[i] * cng * _silu(cg_ref[bi, rows, sl].astype(F32))
                                        ).astype(yc_ref.dtype)

        pl.when(bounded)(functools.partial(rest, True))
        pl.when(jnp.logical_not(bounded))(functools.partial(rest, False))
        return carry

    lax.fori_loop(0, nchunk, chunk, 0)


def _recurrent(proj, grow, lb, ang, cw, cb, gbc, cng, bsz, s):
    assert A_CH == C_CH
    proj3 = proj.reshape(bsz, s, N_MAIN)
    nchunk = R_BLK // C_CH
    col = lambda cid: pl.BlockSpec((bsz, R_BLK, GROUP), lambda t, cid=cid: (0, t, cid))
    full = lambda shp: pl.BlockSpec(shp, lambda t: (0,) * len(shp))
    triu = jnp.triu(jnp.ones((C_CH, C_CH), F32))
    mean_mat = jnp.full((HD, HD), 1.0 / HD, BF16)
    out = pl.BlockSpec((bsz, R_BLK, GROUP), lambda t: (0, t, 0))
    ya, yc = pl.pallas_call(
        _rec_body,
        grid=(s // R_BLK,),
        in_specs=[col(C_AQ), col(C_AF), col(C_AI), col(C_AG),
                  col(C_CQ), col(C_CK), col(C_CV), col(C_CO), col(C_CG),
                  pl.BlockSpec((bsz, nchunk, 8, LANES), lambda t: (0, t, 0, 0)),
                  full((1, GROUP)), full((1, HD)),
                  full((CONV_K, 2 * GROUP)), full((1, 2 * GROUP)),
                  full((8, 1)), full((1, HD)),
                  full((C_CH, C_CH)), full((HD, HD))],
        out_specs=[out, out],
        out_shape=[jax.ShapeDtypeStruct((bsz, s, GROUP), BF16)] * 2,
        scratch_shapes=[pltpu.VMEM((bsz, HEADS, HD, HD), F32),
                        pltpu.VMEM((bsz, R_BLK + 8, GROUP), F32), pltpu.VMEM((bsz, R_BLK + 8, GROUP), F32),
                        pltpu.VMEM((bsz, R_BLK, GROUP), F32),
                        pltpu.VMEM((bsz, nchunk, GROUP, C_CH), F32), pltpu.VMEM((bsz, nchunk, GROUP, C_CH), F32),
                        pltpu.VMEM((bsz, nchunk, 8, LANES), F32), pltpu.VMEM((bsz, nchunk, 8, LANES), F32),
                        pltpu.VMEM((bsz, HEADS, HD, 2 * HD), F32), pltpu.VMEM((bsz, 8, LANES), F32)],
        compiler_params=_cp(("arbitrary",), 56),
        name="hgrn2_mlstm",
    )(*([proj3] * 9), grow.reshape(bsz, s // LANES, 8, LANES),
      lb, ang, cw, cb, gbc, cng, triu, mean_mat)
    return ya.reshape(bsz * s, GROUP), yc.reshape(bsz * s, GROUP)


def _bpre_body(bq_ref, bk_ref, bv_ref, cos_ref, sin_ref, gq_ref, gk_ref, bd_ref,
               qt_ref, k_ref, vt_ref):
    reps = GROUP // LANES
    cosm = jnp.concatenate([cos_ref[...]] * reps, axis=1)
    sinm = jnp.concatenate([sin_ref[...]] * reps, axis=1)
    lane = lax.broadcasted_iota(jnp.int32, (P_BM, GROUP), 1)
    first_half = (lane % ROPE_DIM) < ROPE_DIM // 2
    bd = bd_ref[...]

    def prep(x, g):
        ms = _bdot((x * x).astype(BF16), bd)
        xn = x * lax.rsqrt(ms + EPS) * g
        partner = jnp.where(first_half,
                            pltpu.roll(xn, GROUP - ROPE_DIM // 2, 1),
                            pltpu.roll(xn, ROPE_DIM // 2, 1))
        return xn * cosm + partner * sinm

    q = prep(bq_ref[...].astype(F32), gq_ref[...]) * (B_DQK ** -0.5 * LOG2E)
    k = prep(bk_ref[...].astype(F32), gk_ref[...])
    qt_ref[0] = q.T.astype(BF16)
    k_ref[...] = k.astype(BF16)
    vt_ref[0, 0] = bv_ref[...].astype(F32).T.astype(BF16)


def _bpre(proj, cosm, sinm, gq, gk, bsz, s):
    nb = s // P_BM
    ppk = BK // P_BM
    t = bsz * s
    col = lambda cid: pl.BlockSpec((P_BM, GROUP), lambda i, cid=cid: (i, cid))
    seg = jnp.arange(GROUP) // B_DQK
    bd = (jnp.where(seg[:, None] == seg[None, :], 1.0 / B_DQK, 0.0)).astype(BF16)
    return pl.pallas_call(
        _bpre_body,
        grid=(t // P_BM,),
        in_specs=[col(C_BQ), col(C_BK), col(C_BV),
                  pl.BlockSpec((P_BM, LANES), lambda i: (i, 0)),
                  pl.BlockSpec((P_BM, LANES), lambda i: (i, 0)),
                  pl.BlockSpec((1, GROUP), lambda i: (0, 0)),
                  pl.BlockSpec((1, GROUP), lambda i: (0, 0)),
                  pl.BlockSpec((GROUP, GROUP), lambda i: (0, 0))],
        out_specs=[pl.BlockSpec((1, GROUP, P_BM), lambda i: (i // nb, 0, i % nb)),
                   pl.BlockSpec((P_BM, GROUP), lambda i: (i, 0)),
                   pl.BlockSpec((1, 1, GROUP, P_BM),
                                lambda i: (i // nb, (i % nb) // ppk, 0, (i % nb) % ppk))],
        out_shape=[jax.ShapeDtypeStruct((bsz, GROUP, s), BF16),
                   jax.ShapeDtypeStruct((t, GROUP), BF16),
                   jax.ShapeDtypeStruct((bsz, s // BK, GROUP, BK), BF16)],
        compiler_params=_cp(("arbitrary",), 48),
        name="diff_prep",
    )(proj, proj, proj, cosm, sinm, gq, gk, bd)


def _flash_body(qt_ref, k_ref, vt_ref, bg_ref, lam_ref, sg_ref, y_ref,
                qz_s, m_s, l_s, acc_s, *, lam_init, online_max):
    qi = pl.program_id(2)
    qt = qt_ref[0]
    rowi = lax.broadcasted_iota(jnp.int32, (HD, BQ), 0)
    zero = jnp.zeros_like(qt)
    qz_s[0] = jnp.where(rowi < B_DQK, qt, zero)
    qz_s[1] = jnp.where(rowi >= B_DQK, qt, zero)
    m_s[...] = jnp.full_like(m_s, NEG)
    l_s[...] = jnp.zeros_like(l_s)
    acc_s[...] = jnp.zeros_like(acc_s)

    def step(kblk, vblk, q0, mask):
        qs = slice(q0, BQ)
        scores = [_bdot(kblk, qz_s[c, :, qs]) for c in range(2)]
        for c in range(2):
            s = scores[c]
            if mask is not None:
                s = jnp.where(mask, s, NEG)
            if online_max:
                m_old = m_s[c, :, qs]
                m_new = jnp.maximum(m_old, jnp.max(s, axis=0, keepdims=True))
                alpha = jnp.exp2(m_old - m_new)
                p = jnp.exp2(s - m_new)
                l_s[c, :, qs] = alpha * l_s[c, :, qs] + jnp.sum(p, axis=0, keepdims=True)
                acc_s[c, :, qs] = alpha * acc_s[c, :, qs] + _bdot(vblk, p.astype(BF16))
                m_s[c, :, qs] = m_new
            else:
                p = jnp.exp2(s)
                l_s[c, :, qs] = l_s[c, :, qs] + jnp.sum(p, axis=0, keepdims=True)
                acc_s[c, :, qs] = acc_s[c, :, qs] + _bdot(vblk, p.astype(BF16))

    def body(kb, carry):
        r0 = pl.multiple_of(kb * BK, BK)
        step(k_ref[pl.ds(r0, BK), :], vt_ref[0, kb], 0, None)
        return carry

    lax.fori_loop(0, qi, body, 0)
    r0 = pl.multiple_of(qi * BK, BK)
    vdiag = vt_ref[0, qi]
    for d in range(BK // BKD):
        q0 = d * BKD
        kidx = lax.broadcasted_iota(jnp.int32, (BKD, BQ - q0), 0)
        qidx = lax.broadcasted_iota(jnp.int32, (BKD, BQ - q0), 1)
        rows = pl.ds(pl.multiple_of(r0 + q0, BKD), BKD)
        step(k_ref[rows, :], vdiag[:, q0:q0 + BKD], q0, kidx <= qidx)

    lp = lam_ref[...]
    lam = (jnp.exp(jnp.sum(lp[0:1] * lp[1:2], axis=1, keepdims=True))
           - jnp.exp(jnp.sum(lp[2:3] * lp[3:4], axis=1, keepdims=True)) + lam_init)
    ot = acc_s[0] / l_s[0] - lam * (acc_s[1] / l_s[1])
    o = ot.T
    y = _rms(o, sg_ref[...]) * (1.0 - lam_init) * _silu(bg_ref[...].astype(F32))
    y_ref[...] = y.astype(y_ref.dtype)


def _flash_call(qt, kk, vt, proj, lam_p, sg, *, bsz, s, lam_init, online_max):
    nq = s // BQ
    return pl.pallas_call(
        functools.partial(_flash_body, lam_init=lam_init, online_max=online_max),
        grid=(bsz, HEADS, nq),
        in_specs=[pl.BlockSpec((1, HD, BQ), lambda b, h, q: (b, h, q)),
                  pl.BlockSpec((s, HD), lambda b, h, q: (b, h)),
                  pl.BlockSpec((1, s // BK, HD, BK), lambda b, h, q: (b, 0, h, 0)),
                  pl.BlockSpec((BQ, HD), lambda b, h, q: (b * nq + q, C_BG * HEADS + h)),
                  pl.BlockSpec((4, B_DQK), lambda b, h, q: (0, 0)),
                  pl.BlockSpec((1, HD), lambda b, h, q: (0, 0))],
        out_specs=pl.BlockSpec((BQ, HD), lambda b, h, q: (b * nq + q, h)),
        out_shape=jax.ShapeDtypeStruct((bsz * s, GROUP), BF16),
        scratch_shapes=[pltpu.VMEM((2, HD, BQ), BF16), pltpu.VMEM((2, 1, BQ), F32),
                        pltpu.VMEM((2, 1, BQ), F32), pltpu.VMEM((2, HD, BQ), F32)],
        compiler_params=_cp(("arbitrary", "arbitrary", "arbitrary"), 48),
        name="diff_flash_online" if online_max else "diff_flash",
    )(qt, kk, vt, proj, lam_p, sg)


def _flash(qt, kk, vt, proj, lam_p, sg, qk_g, bsz, s, lam_init):
    bound = (B_DQK ** 0.5) * LOG2E * jnp.max(jnp.abs(qk_g[0])) * jnp.max(jnp.abs(qk_g[1]))
    args = (qt, kk, vt, proj, lam_p, sg)
    call = functools.partial(_flash_call, bsz=bsz, s=s, lam_init=lam_init)
    return lax.cond(bound <= MAX_UNSHIFTED_LOG2_SCORE - math.log2(s),
                    functools.partial(call, online_max=False),
                    functools.partial(call, online_max=True), *args)


def _memkv_body(mem_ref, g_ref, w_ref, kg_ref, km_ref, vm_ref):
    mn = _rms(mem_ref[0], g_ref[...]).astype(BF16)
    kv = _bdot(mn, w_ref[...])
    for h in range(HEADS):
        sl = slice(h * HD, (h + 1) * HD)
        km_ref[0, :, sl] = _rms(kv[:, sl], kg_ref[...]).astype(BF16)
    vm_ref[0] = kv[:, GROUP:].astype(BF16)


def _memkv(mem, g, w, kg):
    bsz = mem.shape[0]
    return pl.pallas_call(
        _memkv_body,
        grid=(bsz,),
        in_specs=[pl.BlockSpec((1, N_MEM, D_MODEL), lambda b: (b, 0, 0)),
                  pl.BlockSpec((1, D_MODEL), lambda b: (0, 0)),
                  pl.BlockSpec((D_MODEL, 2 * GROUP), lambda b: (0, 0)),
                  pl.BlockSpec((1, HD), lambda b: (0, 0))],
        out_specs=[pl.BlockSpec((1, N_MEM, GROUP), lambda b: (b, 0, 0))] * 2,
        out_shape=[jax.ShapeDtypeStruct((bsz, N_MEM, GROUP), BF16)] * 2,
        compiler_params=_cp(("arbitrary",), 32),
        name="mem_kv",
    )(mem, g, w, kg)


def _merge_body(x_ref, ya_ref, yb_ref, yc_ref, xq_ref, xg_ref, km_ref, vm_ref, qg_ref, w_ref, o_ref):
    acc = x_ref[...]
    acc = acc + _bdot(ya_ref[...], w_ref[0:GROUP, :])
    acc = acc + _bdot(yb_ref[...], w_ref[GROUP:2 * GROUP, :])
    acc = acc + _bdot(yc_ref[...], w_ref[2 * GROUP:3 * GROUP, :])
    qg = qg_ref[...]
    yx = []
    for h in range(HEADS):
        sl = slice(h * HD, (h + 1) * HD)
        qn = (_rms(xq_ref[:, sl].astype(F32), qg) * (HD ** -0.5)).astype(BF16)
        s = lax.dot_general(qn, km_ref[0, :, sl], NT, preferred_element_type=F32)
        p = jnp.exp(s - jnp.max(s, axis=-1, keepdims=True))
        l = jnp.sum(p, axis=-1, keepdims=True)
        ox = _bdot(p.astype(BF16), vm_ref[0, :, sl]) / l
        yx.append((ox * _silu(xg_ref[:, sl].astype(F32))).astype(BF16))
    o_ref[...] = acc + _bdot(jnp.concatenate(yx, axis=1), w_ref[3 * GROUP:4 * GROUP, :])


def _merge(x2, ya, yb, yc, proj, km, vm, qg, w_out, bsz, s):
    nb = s // M_BM
    t = bsz * s
    blk = lambda: pl.BlockSpec((M_BM, GROUP), lambda i: (i, 0))
    col = lambda cid: pl.BlockSpec((M_BM, GROUP), lambda i, cid=cid: (i, cid))
    return pl.pallas_call(
        _merge_body,
        grid=(t // M_BM,),
        in_specs=[pl.BlockSpec((M_BM, D_MODEL), lambda i: (i, 0)),
                  blk(), blk(), blk(), col(C_XQ), col(C_XG),
                  pl.BlockSpec((1, N_MEM, GROUP), lambda i: (i // nb, 0, 0)),
                  pl.BlockSpec((1, N_MEM, GROUP), lambda i: (i // nb, 0, 0)),
                  pl.BlockSpec((1, HD), lambda i: (0, 0)),
                  pl.BlockSpec((4 * GROUP, D_MODEL), lambda i: (0, 0))],
        out_specs=pl.BlockSpec((M_BM, D_MODEL), lambda i: (i, 0)),
        out_shape=jax.ShapeDtypeStruct((t, D_MODEL), F32),
        compiler_params=_cp(("arbitrary",), 48),
        name="xattn_out_proj",
    )(x2, ya, yb, yc, proj, proj, km, vm, qg, w_out)


def _layer(x2, mem, cosm, sinm, layer_idx, lb, norm_g, w_in, mlstm_gate_b, hgrn_norm_g,
           diff_qk_norm_g, diff_lambda, diff_subln_g, mlstm_conv_w, mlstm_conv_b,
           mlstm_norm_g, mem_norm_g, w_mem_kv, xattn_qk_norm_g, w_out, bsz, s):
    g12 = 12 * GROUP
    w_main = jnp.concatenate([w_in[:, :g12], w_in[:, g12 + 2 * HEADS:]], axis=1).astype(BF16)
    wgt = jnp.pad(w_in[:, g12:g12 + 2 * HEADS].T, ((0, 8), (0, 0))).astype(BF16)
    proj, grow = _inproj(x2, norm_g[None, :], w_main, wgt)

    ya, yc = _recurrent(proj, grow, lb[None, :], hgrn_norm_g[None, :], mlstm_conv_w,
                        mlstm_conv_b[None, :], mlstm_gate_b[:, None], mlstm_norm_g[None, :], bsz, s)

    gq = jnp.tile(diff_qk_norm_g[0], GROUP // B_DQK)[None, :]
    gk = jnp.tile(diff_qk_norm_g[1], GROUP // B_DQK)[None, :]
    qt, kk, vt = _bpre(proj, cosm, sinm, gq, gk, bsz, s)
    lam_init = 0.8 - 0.6 * math.exp(-0.3 * layer_idx)
    yb = _flash(qt, kk, vt, proj, diff_lambda, diff_subln_g[None, :], diff_qk_norm_g, bsz, s, lam_init)

    km, vm = _memkv(mem, mem_norm_g[None, :], w_mem_kv.astype(BF16), xattn_qk_norm_g[1][None, :])
    return _merge(x2, ya, yb, yc, proj, km, vm, xattn_qk_norm_g[0][None, :],
                  w_out.astype(BF16), bsz, s)


def kernel(x, mem, positions, norm_g, w_in, mlstm_gate_b, hgrn_lb_logits, hgrn_norm_g,
           diff_qk_norm_g, diff_lambda, diff_subln_g, mlstm_conv_w, mlstm_conv_b,
           mlstm_norm_g, mem_norm_g, w_mem_kv, xattn_qk_norm_g, w_out):
    bsz, s, d = x.shape
    depth = norm_g.shape[0]
    assert d == D_MODEL and s % IN_BM == 0 and mem.shape[1] == N_MEM
    cosm, sinm = _rope_tables(positions)
    sm = jax.nn.softmax(hgrn_lb_logits.astype(F32), axis=0)
    lower_bounds = jnp.cumsum(sm, axis=0) - sm[0]
    x2 = x.reshape(bsz * s, d)
    for l in range(depth):
        x2 = _layer(x2, mem, cosm, sinm, l, lower_bounds[l], norm_g[l], w_in[l], mlstm_gate_b[l],
                    hgrn_norm_g[l], diff_qk_norm_g[l], diff_lambda[l], diff_subln_g[l],
                    mlstm_conv_w[l], mlstm_conv_b[l], mlstm_norm_g[l], mem_norm_g[l],
                    w_mem_kv[l], xattn_qk_norm_g[l], w_out[l], bsz, s)
    return x2.reshape(bsz, s, d)
```

```python
import functools
import math

import jax
import jax.numpy as jnp
from jax import lax
from jax.experimental import pallas as pl
from jax.experimental.pallas import tpu as pltpu

F32 = jnp.float32
BF16 = jnp.bfloat16
HI = lax.Precision.HIGHEST

D_MODEL = 1024
N_MEM = 256
GROUP = 512
HEADS = 4
HD = 128
B_DQK = 64
ROPE_DIM = 16
ROPE_THETA = 500000.0
CONV_K = 4
EPS = 1e-6
NEG = -1e30
TINY = 1e-30
LOG2E = math.log2(math.e)
MAX_UNSHIFTED_LOG2_SCORE = 100.0

LANES = 128
N_MAIN = 15 * GROUP

(C_AQ, C_AF, C_AI, C_AG, C_BQ, C_BK, C_BV, C_BG,
 C_CQ, C_CK, C_CV, C_CO, C_CG, C_XQ, C_XG) = range(15)

IN_BM, IN_BN = 512, 1536
R_BLK = 512
A_CH, A_SUB = 128, 16
A_MAX_SUB_LOG2_DECAY = 60.0
C_CH = 128
P_BM = 512
BQ = 2048
BK = 1024
BKD = 512
M_BM = 512
T_BM = 1024

NT = (((1,), (1,)), ((), ()))
TN = (((0,), (0,)), ((), ()))


def _cp(sem, vmem_mib):
    return pltpu.CompilerParams(dimension_semantics=sem, vmem_limit_bytes=vmem_mib * 2 ** 20)


def _silu(x):
    return x * jax.nn.sigmoid(x)


def _log_sigmoid(x):
    return jnp.minimum(x, 0.0) - jnp.log1p(jnp.exp(-jnp.abs(x)))


def _rms(x, g):
    return x * lax.rsqrt(jnp.mean(x * x, axis=-1, keepdims=True) + EPS) * g


def _bdot(a, b):
    return jnp.dot(a, b, preferred_element_type=F32)


def _unit_rms_heads(xs, mean_ref):
    n = xs[0].shape[0]
    x = jnp.concatenate(xs, axis=0)
    y = x * lax.rsqrt(_bdot((x * x).astype(BF16), mean_ref[...]) + EPS)
    return [y[i * n:(i + 1) * n] for i in range(len(xs))]


def _cumsum_rows(x):
    row = lax.broadcasted_iota(jnp.int32, x.shape, 0)
    d = 1
    while d < x.shape[0]:
        x = x + jnp.where(row >= d, pltpu.roll(x, d, 0), 0.0)
        d *= 2
    return x


def _colmat(r):
    return jnp.broadcast_to(r, (r.shape[1], r.shape[1])).T


def _rope_body(pos_ref, invf_ref, cos_ref, sin_ref):
    ang = pos_ref[...].astype(F32) * invf_ref[...]
    lane = lax.broadcasted_iota(jnp.int32, ang.shape, 1) % B_DQK
    c, s = jnp.cos(ang), jnp.sin(ang)
    cos_ref[...] = jnp.where(lane < ROPE_DIM, c, 1.0)
    sin_ref[...] = jnp.where(lane < ROPE_DIM // 2, -s, jnp.where(lane < ROPE_DIM, s, 0.0))


def _rope_tables(positions):
    t = positions.size
    inv_freq = ROPE_THETA ** (-jnp.arange(0, ROPE_DIM, 2, dtype=F32) / ROPE_DIM)
    invf = jnp.tile(inv_freq, LANES // (ROPE_DIM // 2))[None, :]
    return pl.pallas_call(
        _rope_body,
        grid=(t // T_BM,),
        in_specs=[pl.BlockSpec((T_BM, 1), lambda i: (i, 0)),
                  pl.BlockSpec((1, LANES), lambda i: (0, 0))],
        out_specs=[pl.BlockSpec((T_BM, LANES), lambda i: (i, 0))] * 2,
        out_shape=[jax.ShapeDtypeStruct((t, LANES), F32)] * 2,
        compiler_params=_cp(("arbitrary",), 32),
        name="rope_tables",
    )(positions.reshape(t, 1), invf)


def _inproj_body(x_ref, g_ref, w_ref, wgt_ref, o_ref, gr_ref):
    h = _rms(x_ref[...], g_ref[...]).astype(BF16)
    gr = lax.dot_general(wgt_ref[...], h, NT, preferred_element_type=F32)
    for j in range(IN_BM // LANES):
        gr_ref[j] = gr[:8, j * LANES:(j + 1) * LANES]
    for j in range(N_MAIN // IN_BN):
        cols = slice(j * IN_BN, (j + 1) * IN_BN)
        o_ref[:, cols] = _bdot(h, w_ref[:, cols]).astype(o_ref.dtype)


def _inproj(x2, g, w_main, wgt):
    t = x2.shape[0]
    return pl.pallas_call(
        _inproj_body,
        grid=(t // IN_BM,),
        in_specs=[pl.BlockSpec((IN_BM, D_MODEL), lambda i: (i, 0)),
                  pl.BlockSpec((1, D_MODEL), lambda i: (0, 0)),
                  pl.BlockSpec((D_MODEL, N_MAIN), lambda i: (0, 0), pipeline_mode=pl.Buffered(1)),
                  pl.BlockSpec((16, D_MODEL), lambda i: (0, 0))],
        out_specs=[pl.BlockSpec((IN_BM, N_MAIN), lambda i: (i, 0)),
                   pl.BlockSpec((IN_BM // LANES, 8, LANES), lambda i: (i, 0, 0))],
        out_shape=[jax.ShapeDtypeStruct((t, N_MAIN), BF16),
                   jax.ShapeDtypeStruct((t // LANES, 8, LANES), F32)],
        compiler_params=_cp(("arbitrary",), 56),
        name="in_proj",
    )(x2, g, w_main, wgt)


def _hgrn_gates(rows, af_ref, lb, oml, lf_s, ka_s):
    sg = jax.nn.sigmoid(af_ref[rows, :].astype(F32))
    logf = jnp.log2(jnp.maximum(lb + oml * sg, TINY))
    lf_s[...] = logf
    ka_s[...] = oml * (1.0 - sg)
    sums = jnp.sum(logf.reshape(A_CH // A_SUB, A_SUB, GROUP), axis=1)
    return -jnp.min(sums)


def _hgrn_prep(rows, aq_ref, ai_ref, ag_ref, lf_s, ka_s):
    q = _silu(aq_ref[rows, :].astype(F32)) * (HD ** -0.5)
    v = ai_ref[rows, :].astype(F32)
    gate = _silu(ag_ref[rows, :].astype(F32))
    return q, ka_s[...], v, gate, _cumsum_rows(lf_s[...])


def _hgrn_finish(pre, st_ref, vt_ref, bounded):
    q, ka, v, gate, b = pre
    nsub = A_CH // A_SUB
    half = A_SUB // 2
    row = lax.broadcasted_iota(jnp.int32, (A_CH, HD), 0)
    tsub = lax.broadcasted_iota(jnp.int32, (nsub, half, HD), 1)
    ti = lax.broadcasted_iota(jnp.int32, (A_CH, A_CH), 0)
    si = lax.broadcasted_iota(jnp.int32, (A_CH, A_CH), 1)
    blast = b[A_CH - 1:A_CH, :]
    qdec = q * jnp.exp2(b)
    kdec = ka * jnp.exp2(blast - b)
    sdec = jnp.exp2(blast)
    outs = []
    for h in range(HEADS):
        sl = slice(h * HD, (h + 1) * HD)
        qh, kh, vh, bh = q[:, sl], ka[:, sl], v[:, sl], b[:, sl]
        vhb = vh.astype(BF16)
        st = st_ref[h]
        o = lax.dot_general(qdec[:, sl].astype(BF16), st.astype(BF16), NT, preferred_element_type=F32)
        parts = []
        for i in range(nsub):
            lo = i * A_SUB
            hi = lo + A_SUB if bounded else lo
            if hi == 0:
                parts.append(jnp.zeros((A_SUB, A_CH), F32))
                continue
            bi = bh[lo - 1:lo, :] if i else jnp.zeros((1, HD), F32)
            qi = (qh[lo:lo + A_SUB] * jnp.exp2(bh[lo:lo + A_SUB] - bi)).astype(BF16)
            kp = (kh[:hi] * jnp.exp2(bi - bh[:hi])).astype(BF16)
            part = lax.dot_general(qi, kp, NT, preferred_element_type=F32)
            if hi < A_CH:
                part = jnp.concatenate([part, jnp.zeros((A_SUB, A_CH - hi), F32)], axis=1)
            parts.append(part)
        amat = jnp.concatenate(parts, axis=0)
        if bounded:
            amat = jnp.where(si <= ti, amat, 0.0)
        o = o + _bdot(amat.astype(BF16), vhb)
        if not bounded:
            q3 = qh.reshape(nsub, A_SUB, HD)
            k3 = kh.reshape(nsub, A_SUB, HD)
            b3 = bh.reshape(nsub, A_SUB, HD)
            v3 = vh.reshape(nsub, A_SUB, HD)
            q_lo, q_hi = q3[:, :half], q3[:, half:]
            b_lo, b_hi = b3[:, :half], b3[:, half:]
            acc_lo = jnp.zeros((nsub, half, HD), F32)
            acc_hi = jnp.zeros((nsub, half, HD), F32)
            for s in range(A_SUB):
                bs, ks, vs = b3[:, s:s + 1, :], k3[:, s:s + 1, :], v3[:, s:s + 1, :]
                if s < half:
                    e_lo = jnp.exp2(jnp.where(tsub >= s, b_lo - bs, NEG))
                    acc_lo = acc_lo + jnp.sum(q_lo * ks * e_lo, axis=-1, keepdims=True) * vs
                    e_hi = jnp.exp2(b_hi - bs)
                else:
                    e_hi = jnp.exp2(jnp.where(tsub >= s - half, b_hi - bs, NEG))
                acc_hi = acc_hi + jnp.sum(q_hi * ks * e_hi, axis=-1, keepdims=True) * vs
            o = o + jnp.concatenate([acc_lo, acc_hi], axis=1).reshape(A_CH, HD)
        st_ref[h] = st * sdec[:, sl] + _bdot(vt_ref[sl, :].astype(BF16), kdec[:, sl].astype(BF16))
        outs.append(o)
    return outs


def _mlstm_gate_rows(grow_ref, gbc, triu_ref):
    nc = grow_ref.shape[0]
    g = grow_ref[...] + gbc
    srow = lax.broadcasted_iota(jnp.int32, g.shape, 1)
    g = jnp.where(srow < HEADS, g, _log_sigmoid(g)) * LOG2E
    cum = jnp.dot(g.reshape(nc * 8, LANES), triu_ref[...], precision=HI,
                  preferred_element_type=F32).reshape(nc, 8, LANES)
    b = cum[:, HEADS:, :]
    ab = jnp.concatenate([g[:, :HEADS, :] - b, b], axis=1)
    cm = ab.reshape(nc * 8, LANES)
    lane = lax.broadcasted_iota(jnp.int32, cm.shape, 1)
    d = 1
    while d < LANES:
        cm = jnp.maximum(cm, jnp.where(lane >= d, pltpu.roll(cm, d, 1), NEG))
        d *= 2
    return ab, cm.reshape(nc, 8, LANES)


def _mlstm_vectors(ab_ref, cm_ref, m_s):
    ab = ab_ref[...]
    a4, b4 = ab[:HEADS], ab[HEADS:]
    m_prev = m_s[0:HEADS, :]
    mx4 = jnp.maximum(cm_ref[0:HEADS, :], m_prev)
    mt4 = b4 + mx4
    mx_last = jnp.broadcast_to(mx4[:, C_CH - 1:C_CH], (HEADS, LANES))
    wg4 = jnp.exp2(a4 - mx_last)
    dec4 = jnp.exp2(m_prev - mx_last)
    m_s[0:HEADS, :] = jnp.broadcast_to(mt4[:, C_CH - 1:C_CH], (HEADS, LANES))
    cols = [(_colmat(mx4[h:h + 1]), _colmat(mt4[h:h + 1])) for h in range(HEADS)]
    return a4, wg4, dec4, m_prev, cols


def _mlstm_chunk(rows, per_batch, vecs):
    ri = lax.broadcasted_iota(jnp.int32, (C_CH, C_CH), 0)
    ci = lax.broadcasted_iota(jnp.int32, (C_CH, C_CH), 1)
    causal = ri >= ci
    ones_blk = jnp.ones((C_CH, HD), BF16)
    hcs = []
    for (cv_ref, co_ref, qc_s, kt_ref, ct_s), (a4, wg4, dec4, m_prev, cols) in zip(per_batch, vecs):
        for h in range(HEADS):
            sl = slice(h * HD, (h + 1) * HD)
            row = slice(h, h + 1)
            mxm, mtm = cols[h]
            ct = ct_s[h]
            qf = qc_s[rows, sl]
            kt = kt_ref[sl, :]
            vaug = jnp.concatenate([cv_ref[rows, sl], ones_blk], axis=1)

            w = jnp.exp2(jnp.where(causal, a4[row] - mxm, NEG))
            sc = _bdot(qf.astype(BF16), kt.astype(BF16))
            lhs = jnp.concatenate([(sc * w).astype(BF16),
                                   (qf * jnp.exp2(m_prev[row] - mxm)).astype(BF16)], axis=1)
            tot = _bdot(lhs, jnp.concatenate([vaug, ct.astype(BF16)], axis=0))
            hout = tot[:, :HD] / jnp.maximum(jnp.abs(tot[:, HD:]), jnp.exp2(-mtm))

            dec2 = jnp.concatenate([dec4[row], dec4[row]], axis=1)
            ct_s[h] = dec2 * ct + _bdot((kt * wg4[row]).astype(BF16), vaug)
            hcs.append(jax.nn.sigmoid(co_ref[rows, sl].astype(F32)) * hout)
    return hcs


def _rec_body(aq_ref, af_ref, ai_ref, ag_ref, cq_ref, ck_ref, cv_ref, co_ref, cg_ref, grow_ref,
              lb_ref, ang_ref, cw_ref, cb_ref, gbc_ref, cng_ref, triu_ref, mean_ref,
              ya_ref, yc_ref, st_s, xq_s, xk_s, qc_s, kt_s, vt_s, ab_s, cm_s, ct_s, m_s, lf_s, ka_s):
    bsz = aq_ref.shape[0]
    nchunk = R_BLK // C_CH

    @pl.when(pl.program_id(0) == 0)
    def _():
        st_s[...] = jnp.zeros_like(st_s)
        xq_s[:, 0:8, :] = jnp.zeros((bsz, 8, GROUP), F32)
        xk_s[:, 0:8, :] = jnp.zeros((bsz, 8, GROUP), F32)
        ct_s[...] = jnp.zeros_like(ct_s)
        m_s[...] = jnp.zeros_like(m_s)

    cw = cw_ref[...]
    cb = cb_ref[...]
    gbc = gbc_ref[...]

    def conv(x, w, bias):
        t = w[0:1] * x
        for j in range(1, CONV_K):
            t = w[j:j + 1] * x + pltpu.roll(t, 1, 0)
        return t[8:] + bias

    for bi in range(bsz):
        xq_s[bi, 8:8 + R_BLK, :] = cq_ref[bi].astype(F32)
        xk_s[bi, 8:8 + R_BLK, :] = ck_ref[bi].astype(F32)
        qc_s[bi] = _silu(conv(xq_s[bi], cw[:, :GROUP], cb[:, :GROUP]))
        kc = _silu(conv(xk_s[bi], cw[:, GROUP:], cb[:, GROUP:])) * (HD ** -0.5)
        for c in range(nchunk):
            kt_s[bi, c] = kc[c * C_CH:(c + 1) * C_CH, :].T
            vt_s[bi, c] = ai_ref[bi, c * C_CH:(c + 1) * C_CH, :].astype(F32).T
        xq_s[bi, 0:8, :] = xq_s[bi, R_BLK:R_BLK + 8, :]
        xk_s[bi, 0:8, :] = xk_s[bi, R_BLK:R_BLK + 8, :]
        ab_s[bi], cm_s[bi] = _mlstm_gate_rows(grow_ref.at[bi], gbc, triu_ref)

    lb = lb_ref[...]
    oml = 1.0 - lb
    ang = ang_ref[...]
    cng = cng_ref[...]

    def chunk(c, carry):
        rows = pl.ds(pl.multiple_of(c * C_CH, C_CH), C_CH)
        spans = [_hgrn_gates(rows, af_ref.at[bi], lb, oml, lf_s.at[bi], ka_s.at[bi]) for bi in range(bsz)]
        bounded = functools.reduce(jnp.maximum, spans) <= A_MAX_SUB_LOG2_DECAY

        def rest(is_bounded):
            vecs = [_mlstm_vectors(ab_s.at[bi, c], cm_s.at[bi, c], m_s.at[bi]) for bi in range(bsz)]
            pres = [_hgrn_prep(rows, aq_ref.at[bi], ai_ref.at[bi], ag_ref.at[bi], lf_s.at[bi], ka_s.at[bi])
                    for bi in range(bsz)]
            outs = [o for bi in range(bsz)
                    for o in _hgrn_finish(pres[bi], st_s.at[bi], vt_s.at[bi, c], is_bounded)]
            outs = _unit_rms_heads(outs, mean_ref)
            per_batch = [(cv_ref.at[bi], co_ref.at[bi], qc_s.at[bi], kt_s.at[bi, c], ct_s.at[bi])
                         for bi in range(bsz)]
            hcs = _unit_rms_heads(_mlstm_chunk(rows, per_batch, vecs), mean_ref)
            for i in range(bsz * HEADS):
                bi, h = divmod(i, HEADS)
                sl = slice(h * HD, (h + 1) * HD)
                ya_ref[bi, rows, sl] = (outs[i] * ang * pres[bi][3][:, sl]).astype(ya_ref.dtype)
                yc_ref[bi, rows, sl] = (hcs[i] * cng * _silu(cg_ref[bi, rows, sl].astype(F32))
                                        ).astype(yc_ref.dtype)

        pl.when(bounded)(functools.partial(rest, True))
        pl.when(jnp.logical_not(bounded))(functools.partial(rest, False))
        return carry

    lax.fori_loop(0, nchunk, chunk, 0)


def _recurrent(proj, grow, lb, ang, cw, cb, gbc, cng, bsz, s):
    assert A_CH == C_CH
    proj3 = proj.reshape(bsz, s, N_MAIN)
    nchunk = R_BLK // C_CH
    col = lambda cid: pl.BlockSpec((bsz, R_BLK, GROUP), lambda t, cid=cid: (0, t, cid))
    full = lambda shp: pl.BlockSpec(shp, lambda t: (0,) * len(shp))
    triu = jnp.triu(jnp.ones((C_CH, C_CH), F32))
    mean_mat = jnp.full((HD, HD), 1.0 / HD, BF16)
    out = pl.BlockSpec((bsz, R_BLK, GROUP), lambda t: (0, t, 0))
    ya, yc = pl.pallas_call(
        _rec_body,
        grid=(s // R_BLK,),
        in_specs=[col(C_AQ), col(C_AF), col(C_AI), col(C_AG),
                  col(C_CQ), col(C_CK), col(C_CV), col(C_CO), col(C_CG),
                  pl.BlockSpec((bsz, nchunk, 8, LANES), lambda t: (0, t, 0, 0)),
                  full((1, GROUP)), full((1, HD)),
                  full((CONV_K, 2 * GROUP)), full((1, 2 * GROUP)),
                  full((8, 1)), full((1, HD)),
                  full((C_CH, C_CH)), full((HD, HD))],
        out_specs=[out, out],
        out_shape=[jax.ShapeDtypeStruct((bsz, s, GROUP), BF16)] * 2,
        scratch_shapes=[pltpu.VMEM((bsz, HEADS, HD, HD), F32),
                        pltpu.VMEM((bsz, R_BLK + 8, GROUP), F32), pltpu.VMEM((bsz, R_BLK + 8, GROUP), F32),
                        pltpu.VMEM((bsz, R_BLK, GROUP), F32),
                        pltpu.VMEM((bsz, nchunk, GROUP, C_CH), F32), pltpu.VMEM((bsz, nchunk, GROUP, C_CH), F32),
                        pltpu.VMEM((bsz, nchunk, 8, LANES), F32), pltpu.VMEM((bsz, nchunk, 8, LANES), F32),
                        pltpu.VMEM((bsz, HEADS, HD, 2 * HD), F32), pltpu.VMEM((bsz, 8, LANES), F32),
                        pltpu.VMEM((bsz, A_CH, GROUP), F32), pltpu.VMEM((bsz, A_CH, GROUP), F32)],
        compiler_params=_cp(("arbitrary",), 56),
        name="hgrn2_mlstm",
    )(*([proj3] * 9), grow.reshape(bsz, s // LANES, 8, LANES),
      lb, ang, cw, cb, gbc, cng, triu, mean_mat)
    return ya.reshape(bsz * s, GROUP), yc.reshape(bsz * s, GROUP)


def _bpre_body(bq_ref, bk_ref, bv_ref, cos_ref, sin_ref, gq_ref, gk_ref, bd_ref,
               qt_ref, k_ref, vt_ref):
    reps = GROUP // LANES
    cosm = jnp.concatenate([cos_ref[...]] * reps, axis=1)
    sinm = jnp.concatenate([sin_ref[...]] * reps, axis=1)
    lane = lax.broadcasted_iota(jnp.int32, (P_BM, GROUP), 1)
    first_half = (lane % ROPE_DIM) < ROPE_DIM // 2
    bd = bd_ref[...]

    def prep(x, g):
        ms = _bdot((x * x).astype(BF16), bd)
        xn = x * lax.rsqrt(ms + EPS) * g
        partner = jnp.where(first_half,
                            pltpu.roll(xn, GROUP - ROPE_DIM // 2, 1),
                            pltpu.roll(xn, ROPE_DIM // 2, 1))
        return xn * cosm + partner * sinm

    q = prep(bq_ref[...].astype(F32), gq_ref[...]) * (B_DQK ** -0.5 * LOG2E)
    k = prep(bk_ref[...].astype(F32), gk_ref[...])
    qt_ref[0] = q.T.astype(BF16)
    k_ref[...] = k.astype(BF16)
    vt_ref[0, 0] = bv_ref[...].astype(F32).T.astype(BF16)


def _bpre(proj, cosm, sinm, gq, gk, bsz, s):
    nb = s // P_BM
    ppk = BK // P_BM
    t = bsz * s
    col = lambda cid: pl.BlockSpec((P_BM, GROUP), lambda i, cid=cid: (i, cid))
    seg = jnp.arange(GROUP) // B_DQK
    bd = (jnp.where(seg[:, None] == seg[None, :], 1.0 / B_DQK, 0.0)).astype(BF16)
    return pl.pallas_call(
        _bpre_body,
        grid=(t // P_BM,),
        in_specs=[col(C_BQ), col(C_BK), col(C_BV),
                  pl.BlockSpec((P_BM, LANES), lambda i: (i, 0)),
                  pl.BlockSpec((P_BM, LANES), lambda i: (i, 0)),
                  pl.BlockSpec((1, GROUP), lambda i: (0, 0)),
                  pl.BlockSpec((1, GROUP), lambda i: (0, 0)),
                  pl.BlockSpec((GROUP, GROUP), lambda i: (0, 0))],
        out_specs=[pl.BlockSpec((1, GROUP, P_BM), lambda i: (i // nb, 0, i % nb)),
                   pl.BlockSpec((P_BM, GROUP), lambda i: (i, 0)),
                   pl.BlockSpec((1, 1, GROUP, P_BM),
                                lambda i: (i // nb, (i % nb) // ppk, 0, (i % nb) % ppk))],
        out_shape=[jax.ShapeDtypeStruct((bsz, GROUP, s), BF16),
                   jax.ShapeDtypeStruct((t, GROUP), BF16),
                   jax.ShapeDtypeStruct((bsz, s // BK, GROUP, BK), BF16)],
        compiler_params=_cp(("arbitrary",), 48),
        name="diff_prep",
    )(proj, proj, proj, cosm, sinm, gq, gk, bd)


def _flash_body(qt_ref, k_ref, vt_ref, bg_ref, lam_ref, sg_ref, y_ref,
                qz_s, m_s, l_s, acc_s, *, lam_init, online_max):
    qi = pl.program_id(2)
    qt = qt_ref[0]
    rowi = lax.broadcasted_iota(jnp.int32, (HD, BQ), 0)
    zero = jnp.zeros_like(qt)
    qz_s[0] = jnp.where(rowi < B_DQK, qt, zero)
    qz_s[1] = jnp.where(rowi >= B_DQK, qt, zero)
    m_s[...] = jnp.full_like(m_s, NEG)
    l_s[...] = jnp.zeros_like(l_s)
    acc_s[...] = jnp.zeros_like(acc_s)

    def step(kblk, vblk, q0, mask):
        qs = slice(q0, BQ)
        scores = [_bdot(kblk, qz_s[c, :, qs]) for c in range(2)]
        for c in range(2):
            s = scores[c]
            if mask is not None:
                s = jnp.where(mask, s, NEG)
            if online_max:
                m_old = m_s[c, :, qs]
                m_new = jnp.maximum(m_old, jnp.max(s, axis=0, keepdims=True))
                alpha = jnp.exp2(m_old - m_new)
                p = jnp.exp2(s - m_new)
                l_s[c, :, qs] = alpha * l_s[c, :, qs] + jnp.sum(p, axis=0, keepdims=True)
                acc_s[c, :, qs] = alpha * acc_s[c, :, qs] + _bdot(vblk, p.astype(BF16))
                m_s[c, :, qs] = m_new
            else:
                p = jnp.exp2(s)
                l_s[c, :, qs] = l_s[c, :, qs] + jnp.sum(p, axis=0, keepdims=True)
                acc_s[c, :, qs] = acc_s[c, :, qs] + _bdot(vblk, p.astype(BF16))

    def body(kb, carry):
        r0 = pl.multiple_of(kb * BK, BK)
        step(k_ref[pl.ds(r0, BK), :], vt_ref[0, kb], 0, None)
        return carry

    kpq = BQ // BK
    lax.fori_loop(0, qi * kpq, body, 0)
    for d in range(BQ // BKD):
        q0 = d * BKD
        kidx = lax.broadcasted_iota(jnp.int32, (BKD, BQ - q0), 0)
        qidx = lax.broadcasted_iota(jnp.int32, (BKD, BQ - q0), 1)
        rows = pl.ds(pl.multiple_of(qi * BQ + q0, BKD), BKD)
        lo = q0 % BK
        vblk = vt_ref[0, qi * kpq + q0 // BK][:, lo:lo + BKD]
        step(k_ref[rows, :], vblk, q0, kidx <= qidx)

    lp = lam_ref[...]
    lam = (jnp.exp(jnp.sum(lp[0:1] * lp[1:2], axis=1, keepdims=True))
           - jnp.exp(jnp.sum(lp[2:3] * lp[3:4], axis=1, keepdims=True)) + lam_init)
    ot = acc_s[0] / l_s[0] - lam * (acc_s[1] / l_s[1])
    o = ot.T
    y = _rms(o, sg_ref[...]) * (1.0 - lam_init) * _silu(bg_ref[...].astype(F32))
    y_ref[...] = y.astype(y_ref.dtype)


def _flash_call(qt, kk, vt, proj, lam_p, sg, *, bsz, s, lam_init, online_max):
    nq = s // BQ
    return pl.pallas_call(
        functools.partial(_flash_body, lam_init=lam_init, online_max=online_max),
        grid=(bsz, HEADS, nq),
        in_specs=[pl.BlockSpec((1, HD, BQ), lambda b, h, q: (b, h, q)),
                  pl.BlockSpec((s, HD), lambda b, h, q: (b, h)),
                  pl.BlockSpec((1, s // BK, HD, BK), lambda b, h, q: (b, 0, h, 0)),
                  pl.BlockSpec((BQ, HD), lambda b, h, q: (b * nq + q, C_BG * HEADS + h)),
                  pl.BlockSpec((4, B_DQK), lambda b, h, q: (0, 0)),
                  pl.BlockSpec((1, HD), lambda b, h, q: (0, 0))],
        out_specs=pl.BlockSpec((BQ, HD), lambda b, h, q: (b * nq + q, h)),
        out_shape=jax.ShapeDtypeStruct((bsz * s, GROUP), BF16),
        scratch_shapes=[pltpu.VMEM((2, HD, BQ), BF16), pltpu.VMEM((2, 1, BQ), F32),
                        pltpu.VMEM((2, 1, BQ), F32), pltpu.VMEM((2, HD, BQ), F32)],
        compiler_params=_cp(("arbitrary", "arbitrary", "arbitrary"), 48),
        name="diff_flash_online" if online_max else "diff_flash",
    )(qt, kk, vt, proj, lam_p, sg)


def _flash(qt, kk, vt, proj, lam_p, sg, qk_g, bsz, s, lam_init):
    bound = (B_DQK ** 0.5) * LOG2E * jnp.max(jnp.abs(qk_g[0])) * jnp.max(jnp.abs(qk_g[1]))
    args = (qt, kk, vt, proj, lam_p, sg)
    call = functools.partial(_flash_call, bsz=bsz, s=s, lam_init=lam_init)
    return lax.cond(bound <= MAX_UNSHIFTED_LOG2_SCORE - math.log2(s),
                    functools.partial(call, online_max=False),
                    functools.partial(call, online_max=True), *args)


def _memkv_body(mem_ref, g_ref, w_ref, kg_ref, km_ref, vm_ref):
    mn = _rms(mem_ref[0], g_ref[...]).astype(BF16)
    kv = _bdot(mn, w_ref[...])
    for h in range(HEADS):
        sl = slice(h * HD, (h + 1) * HD)
        km_ref[0, :, sl] = _rms(kv[:, sl], kg_ref[...]).astype(BF16)
    vm_ref[0] = kv[:, GROUP:].astype(BF16)


def _memkv(mem, g, w, kg):
    bsz = mem.shape[0]
    return pl.pallas_call(
        _memkv_body,
        grid=(bsz,),
        in_specs=[pl.BlockSpec((1, N_MEM, D_MODEL), lambda b: (b, 0, 0)),
                  pl.BlockSpec((1, D_MODEL), lambda b: (0, 0)),
                  pl.BlockSpec((D_MODEL, 2 * GROUP), lambda b: (0, 0)),
                  pl.BlockSpec((1, HD), lambda b: (0, 0))],
        out_specs=[pl.BlockSpec((1, N_MEM, GROUP), lambda b: (b, 0, 0))] * 2,
        out_shape=[jax.ShapeDtypeStruct((bsz, N_MEM, GROUP), BF16)] * 2,
        compiler_params=_cp(("arbitrary",), 32),
        name="mem_kv",
    )(mem, g, w, kg)


def _merge_body(x_ref, ya_ref, yb_ref, yc_ref, xq_ref, xg_ref, km_ref, vm_ref, qg_ref, w_ref, o_ref):
    acc = x_ref[...]
    acc = acc + _bdot(ya_ref[...], w_ref[0:GROUP, :])
    acc = acc + _bdot(yb_ref[...], w_ref[GROUP:2 * GROUP, :])
    acc = acc + _bdot(yc_ref[...], w_ref[2 * GROUP:3 * GROUP, :])
    qg = qg_ref[...]
    yx = []
    for h in range(HEADS):
        sl = slice(h * HD, (h + 1) * HD)
        qn = (_rms(xq_ref[:, sl].astype(F32), qg) * (HD ** -0.5)).astype(BF16)
        s = lax.dot_general(qn, km_ref[0, :, sl], NT, preferred_element_type=F32)
        p = jnp.exp(s - jnp.max(s, axis=-1, keepdims=True))
        l = jnp.sum(p, axis=-1, keepdims=True)
        ox = _bdot(p.astype(BF16), vm_ref[0, :, sl]) / l
        yx.append((ox * _silu(xg_ref[:, sl].astype(F32))).astype(BF16))
    o_ref[...] = acc + _bdot(jnp.concatenate(yx, axis=1), w_ref[3 * GROUP:4 * GROUP, :])


def _merge(x2, ya, yb, yc, proj, km, vm, qg, w_out, bsz, s):
    nb = s // M_BM
    t = bsz * s
    blk = lambda: pl.BlockSpec((M_BM, GROUP), lambda i: (i, 0))
    col = lambda cid: pl.BlockSpec((M_BM, GROUP), lambda i, cid=cid: (i, cid))
    return pl.pallas_call(
        _merge_body,
        grid=(t // M_BM,),
        in_specs=[pl.BlockSpec((M_BM, D_MODEL), lambda i: (i, 0)),
                  blk(), blk(), blk(), col(C_XQ), col(C_XG),
                  pl.BlockSpec((1, N_MEM, GROUP), lambda i: (i // nb, 0, 0)),
                  pl.BlockSpec((1, N_MEM, GROUP), lambda i: (i // nb, 0, 0)),
                  pl.BlockSpec((1, HD), lambda i: (0, 0)),
                  pl.BlockSpec((4 * GROUP, D_MODEL), lambda i: (0, 0))],
        out_specs=pl.BlockSpec((M_BM, D_MODEL), lambda i: (i, 0)),
        out_shape=jax.ShapeDtypeStruct((t, D_MODEL), F32),
        compiler_params=_cp(("arbitrary",), 48),
        name="xattn_out_proj",
    )(x2, ya, yb, yc, proj, proj, km, vm, qg, w_out)


def _layer(x2, mem, cosm, sinm, layer_idx, lb, norm_g, w_in, mlstm_gate_b, hgrn_norm_g,
           diff_qk_norm_g, diff_lambda, diff_subln_g, mlstm_conv_w, mlstm_conv_b,
           mlstm_norm_g, mem_norm_g, w_mem_kv, xattn_qk_norm_g, w_out, bsz, s):
    g12 = 12 * GROUP
    w_main = jnp.concatenate([w_in[:, :g12], w_in[:, g12 + 2 * HEADS:]], axis=1).astype(BF16)
    wgt = jnp.pad(w_in[:, g12:g12 + 2 * HEADS].T, ((0, 8), (0, 0))).astype(BF16)
    proj, grow = _inproj(x2, norm_g[None, :], w_main, wgt)

    ya, yc = _recurrent(proj, grow, lb[None, :], hgrn_norm_g[None, :], mlstm_conv_w,
                        mlstm_conv_b[None, :], mlstm_gate_b[:, None], mlstm_norm_g[None, :], bsz, s)

    gq = jnp.tile(diff_qk_norm_g[0], GROUP // B_DQK)[None, :]
    gk = jnp.tile(diff_qk_norm_g[1], GROUP // B_DQK)[None, :]
    qt, kk, vt = _bpre(proj, cosm, sinm, gq, gk, bsz, s)
    lam_init = 0.8 - 0.6 * math.exp(-0.3 * layer_idx)
    yb = _flash(qt, kk, vt, proj, diff_lambda, diff_subln_g[None, :], diff_qk_norm_g, bsz, s, lam_init)

    km, vm = _memkv(mem, mem_norm_g[None, :], w_mem_kv.astype(BF16), xattn_qk_norm_g[1][None, :])
    return _merge(x2, ya, yb, yc, proj, km, vm, xattn_qk_norm_g[0][None, :],
                  w_out.astype(BF16), bsz, s)


def kernel(x, mem, positions, norm_g, w_in, mlstm_gate_b, hgrn_lb_logits, hgrn_norm_g,
           diff_qk_norm_g, diff_lambda, diff_subln_g, mlstm_conv_w, mlstm_conv_b,
           mlstm_norm_g, mem_norm_g, w_mem_kv, xattn_qk_norm_g, w_out):
    bsz, s, d = x.shape
    depth = norm_g.shape[0]
    assert d == D_MODEL and mem.shape[1] == N_MEM
    assert all(s % blk == 0 for blk in (IN_BM, R_BLK, P_BM, BQ, BK, M_BM)) and (bsz * s) % T_BM == 0
    cosm, sinm = _rope_tables(positions)
    sm = jax.nn.softmax(hgrn_lb_logits.astype(F32), axis=0)
    lower_bounds = jnp.cumsum(sm, axis=0) - sm[0]
    x2 = x.reshape(bsz * s, d)
    for l in range(depth):
        x2 = _layer(x2, mem, cosm, sinm, l, lower_bounds[l], norm_g[l], w_in[l], mlstm_gate_b[l],
                    hgrn_norm_g[l], diff_qk_norm_g[l], diff_lambda[l], diff_subln_g[l],
                    mlstm_conv_w[l], mlstm_conv_b[l], mlstm_norm_g[l], mem_norm_g[l],
                    w_mem_kv[l], xattn_qk_norm_g[l], w_out[l], bsz, s)
    return x2.reshape(bsz, s, d)
```

```python
import functools
import math

import jax
import jax.numpy as jnp
from jax import lax
from jax.experimental import pallas as pl
from jax.experimental.pallas import tpu as pltpu

F32 = jnp.float32
BF16 = jnp.bfloat16
HI = lax.Precision.HIGHEST

D_MODEL = 1024
N_MEM = 256
GROUP = 512
HEADS = 4
HD = 128
B_DQK = 64
ROPE_DIM = 16
ROPE_THETA = 500000.0
CONV_K = 4
EPS = 1e-6
NEG = -1e30
TINY = 1e-30
LOG2E = math.log2(math.e)
MAX_UNSHIFTED_LOG2_SCORE = 100.0

LANES = 128
N_MAIN = 15 * GROUP

(W_AQ, W_AF, W_AI, W_AG, W_BQ, W_BK, W_BV, W_BG,
 W_CQ, W_CK, W_CV, W_CO, W_CG, W_XQ, W_XG) = range(15)
(C_AQ, C_AF, C_AI, C_AG, C_BG, C_CQ, C_CK, C_CV, C_CO, C_CG, C_XQ, C_XG) = range(12)
OUT_COL = {W_AQ: C_AQ, W_AF: C_AF, W_AI: C_AI, W_AG: C_AG, W_BG: C_BG, W_CQ: C_CQ, W_CK: C_CK,
           W_CV: C_CV, W_CO: C_CO, W_CG: C_CG, W_XQ: C_XQ, W_XG: C_XG}
N_OUT = len(OUT_COL) * GROUP

IN_BM, IN_BN = 512, 1536
R_BLK = 512
A_CH, A_SUB = 128, 16
A_MAX_SUB_LOG2_DECAY = 60.0
C_CH = 128
P_BM = 512
BQ = 2048
BK = 1024
BKD = 512
M_BM = 512
T_BM = 1024

NT = (((1,), (1,)), ((), ()))
TN = (((0,), (0,)), ((), ()))


def _cp(sem, vmem_mib):
    return pltpu.CompilerParams(dimension_semantics=sem, vmem_limit_bytes=vmem_mib * 2 ** 20)


def _silu(x):
    return x * jax.nn.sigmoid(x)


def _log_sigmoid(x):
    return jnp.minimum(x, 0.0) - jnp.log1p(jnp.exp(-jnp.abs(x)))


def _rms(x, g):
    return x * lax.rsqrt(jnp.mean(x * x, axis=-1, keepdims=True) + EPS) * g


def _bdot(a, b):
    return jnp.dot(a, b, preferred_element_type=F32)


def _unit_rms_heads(xs, mean_ref):
    n = xs[0].shape[0]
    x = jnp.concatenate(xs, axis=0)
    y = x * lax.rsqrt(_bdot((x * x).astype(BF16), mean_ref[...]) + EPS)
    return [y[i * n:(i + 1) * n] for i in range(len(xs))]


def _cumsum_rows(x):
    row = lax.broadcasted_iota(jnp.int32, x.shape, 0)
    d = 1
    while d < x.shape[0]:
        x = x + jnp.where(row >= d, pltpu.roll(x, d, 0), 0.0)
        d *= 2
    return x


def _colmat(r):
    return jnp.broadcast_to(r, (r.shape[1], r.shape[1])).T


def _rope_body(pos_ref, invf_ref, cos_ref, sin_ref):
    ang = pos_ref[...].astype(F32) * invf_ref[...]
    lane = lax.broadcasted_iota(jnp.int32, ang.shape, 1) % B_DQK
    c, s = jnp.cos(ang), jnp.sin(ang)
    cos_ref[...] = jnp.where(lane < ROPE_DIM, c, 1.0)
    sin_ref[...] = jnp.where(lane < ROPE_DIM // 2, -s, jnp.where(lane < ROPE_DIM, s, 0.0))


def _rope_tables(positions):
    t = positions.size
    inv_freq = ROPE_THETA ** (-jnp.arange(0, ROPE_DIM, 2, dtype=F32) / ROPE_DIM)
    invf = jnp.tile(inv_freq, LANES // (ROPE_DIM // 2))[None, :]
    return pl.pallas_call(
        _rope_body,
        grid=(t // T_BM,),
        in_specs=[pl.BlockSpec((T_BM, 1), lambda i: (i, 0)),
                  pl.BlockSpec((1, LANES), lambda i: (0, 0))],
        out_specs=[pl.BlockSpec((T_BM, LANES), lambda i: (i, 0))] * 2,
        out_shape=[jax.ShapeDtypeStruct((t, LANES), F32)] * 2,
        compiler_params=_cp(("arbitrary",), 32),
        name="rope_tables",
    )(positions.reshape(t, 1), invf)


def _inproj_body(x_ref, g_ref, w_ref, wgt_ref, cos_ref, sin_ref, gq_ref, gk_ref, bd_ref,
                 o_ref, gr_ref, qt_ref, k_ref, vt_ref, tmp_ref):
    h = _rms(x_ref[...], g_ref[...]).astype(BF16)
    gr = lax.dot_general(wgt_ref[...], h, NT, preferred_element_type=F32)
    for j in range(IN_BM // LANES):
        gr_ref[j] = gr[:8, j * LANES:(j + 1) * LANES]

    reps = GROUP // LANES
    cosm = jnp.concatenate([cos_ref[...]] * reps, axis=1)
    sinm = jnp.concatenate([sin_ref[...]] * reps, axis=1)
    lane = lax.broadcasted_iota(jnp.int32, (IN_BM, GROUP), 1)
    first_half = (lane % ROPE_DIM) < ROPE_DIM // 2

    def norm_rope(x, gain):
        ms = _bdot((x * x).astype(BF16), bd_ref[...])
        xn = x * lax.rsqrt(ms + EPS) * gain
        partner = jnp.where(first_half,
                            pltpu.roll(xn, GROUP - ROPE_DIM // 2, 1),
                            pltpu.roll(xn, ROPE_DIM // 2, 1))
        return xn * cosm + partner * sinm

    for wid in range(N_MAIN // GROUP):
        y = _bdot(h, w_ref[:, wid * GROUP:(wid + 1) * GROUP])
        if wid == W_BQ:
            tmp_ref[...] = norm_rope(y, gq_ref[...]) * (B_DQK ** -0.5 * LOG2E)
            qt_ref[0] = tmp_ref[...].T.astype(BF16)
        elif wid == W_BK:
            k_ref[...] = norm_rope(y, gk_ref[...]).astype(BF16)
        elif wid == W_BV:
            tmp_ref[...] = y
            vt_ref[0, 0] = tmp_ref[...].T.astype(BF16)
        else:
            cid = OUT_COL[wid]
            o_ref[:, cid * GROUP:(cid + 1) * GROUP] = y.astype(o_ref.dtype)


def _inproj(x2, g, w_main, wgt, cosm, sinm, gq, gk, bsz, s):
    assert IN_BM == P_BM
    t = x2.shape[0]
    nb = s // P_BM
    ppk = BK // P_BM
    seg = jnp.arange(GROUP) // B_DQK
    bd = (jnp.where(seg[:, None] == seg[None, :], 1.0 / B_DQK, 0.0)).astype(BF16)
    return pl.pallas_call(
        _inproj_body,
        grid=(t // IN_BM,),
        in_specs=[pl.BlockSpec((IN_BM, D_MODEL), lambda i: (i, 0)),
                  pl.BlockSpec((1, D_MODEL), lambda i: (0, 0)),
                  pl.BlockSpec((D_MODEL, N_MAIN), lambda i: (0, 0), pipeline_mode=pl.Buffered(1)),
                  pl.BlockSpec((16, D_MODEL), lambda i: (0, 0)),
                  pl.BlockSpec((IN_BM, LANES), lambda i: (i, 0)),
                  pl.BlockSpec((IN_BM, LANES), lambda i: (i, 0)),
                  pl.BlockSpec((1, GROUP), lambda i: (0, 0)),
                  pl.BlockSpec((1, GROUP), lambda i: (0, 0)),
                  pl.BlockSpec((GROUP, GROUP), lambda i: (0, 0))],
        out_specs=[pl.BlockSpec((IN_BM, N_OUT), lambda i: (i, 0)),
                   pl.BlockSpec((IN_BM // LANES, 8, LANES), lambda i: (i, 0, 0)),
                   pl.BlockSpec((1, GROUP, P_BM), lambda i: (i // nb, 0, i % nb)),
                   pl.BlockSpec((P_BM, GROUP), lambda i: (i, 0)),
                   pl.BlockSpec((1, 1, GROUP, P_BM),
                                lambda i: (i // nb, (i % nb) // ppk, 0, (i % nb) % ppk))],
        out_shape=[jax.ShapeDtypeStruct((t, N_OUT), BF16),
                   jax.ShapeDtypeStruct((t // LANES, 8, LANES), F32),
                   jax.ShapeDtypeStruct((bsz, GROUP, s), BF16),
                   jax.ShapeDtypeStruct((t, GROUP), BF16),
                   jax.ShapeDtypeStruct((bsz, s // BK, GROUP, BK), BF16)],
        scratch_shapes=[pltpu.VMEM((IN_BM, GROUP), F32)],
        compiler_params=_cp(("arbitrary",), 56),
        name="in_proj",
    )(x2, g, w_main, wgt, cosm, sinm, gq, gk, bd)


def _hgrn_gates(rows, af_ref, lb, oml, lf_s, ka_s):
    sg = jax.nn.sigmoid(af_ref[rows, :].astype(F32))
    logf = jnp.log2(jnp.maximum(lb + oml * sg, TINY))
    lf_s[...] = logf
    ka_s[...] = oml * (1.0 - sg)
    sums = jnp.sum(logf.reshape(A_CH // A_SUB, A_SUB, GROUP), axis=1)
    return -jnp.min(sums)


def _hgrn_prep(rows, aq_ref, ai_ref, ag_ref, lf_s, ka_s):
    q = _silu(aq_ref[rows, :].astype(F32)) * (HD ** -0.5)
    v = ai_ref[rows, :].astype(F32)
    gate = _silu(ag_ref[rows, :].astype(F32))
    return q, ka_s[...], v, gate, _cumsum_rows(lf_s[...])


def _hgrn_finish(pre, st_ref, vt_ref, bounded):
    q, ka, v, gate, b = pre
    nsub = A_CH // A_SUB
    half = A_SUB // 2
    row = lax.broadcasted_iota(jnp.int32, (A_CH, HD), 0)
    tsub = lax.broadcasted_iota(jnp.int32, (nsub, half, HD), 1)
    ti = lax.broadcasted_iota(jnp.int32, (A_CH, A_CH), 0)
    si = lax.broadcasted_iota(jnp.int32, (A_CH, A_CH), 1)
    blast = b[A_CH - 1:A_CH, :]
    qdec = q * jnp.exp2(b)
    kdec = ka * jnp.exp2(blast - b)
    sdec = jnp.exp2(blast)
    outs = []
    for h in range(HEADS):
        sl = slice(h * HD, (h + 1) * HD)
        qh, kh, vh, bh = q[:, sl], ka[:, sl], v[:, sl], b[:, sl]
        vhb = vh.astype(BF16)
        st = st_ref[h]
        o = lax.dot_general(qdec[:, sl].astype(BF16), st.astype(BF16), NT, preferred_element_type=F32)
        parts = []
        for i in range(nsub):
            lo = i * A_SUB
            hi = lo + A_SUB if bounded else lo
            if hi == 0:
                parts.append(jnp.zeros((A_SUB, A_CH), F32))
                continue
            bi = bh[lo - 1:lo, :] if i else jnp.zeros((1, HD), F32)
            qi = (qh[lo:lo + A_SUB] * jnp.exp2(bh[lo:lo + A_SUB] - bi)).astype(BF16)
            kp = (kh[:hi] * jnp.exp2(bi - bh[:hi])).astype(BF16)
            part = lax.dot_general(qi, kp, NT, preferred_element_type=F32)
            if hi < A_CH:
                part = jnp.concatenate([part, jnp.zeros((A_SUB, A_CH - hi), F32)], axis=1)
            parts.append(part)
        amat = jnp.concatenate(parts, axis=0)
        if bounded:
            amat = jnp.where(si <= ti, amat, 0.0)
        o = o + _bdot(amat.astype(BF16), vhb)
        if not bounded:
            q3 = qh.reshape(nsub, A_SUB, HD)
            k3 = kh.reshape(nsub, A_SUB, HD)
            b3 = bh.reshape(nsub, A_SUB, HD)
            v3 = vh.reshape(nsub, A_SUB, HD)
            q_lo, q_hi = q3[:, :half], q3[:, half:]
            b_lo, b_hi = b3[:, :half], b3[:, half:]
            acc_lo = jnp.zeros((nsub, half, HD), F32)
            acc_hi = jnp.zeros((nsub, half, HD), F32)
            for s in range(A_SUB):
                bs, ks, vs = b3[:, s:s + 1, :], k3[:, s:s + 1, :], v3[:, s:s + 1, :]
                if s < half:
                    e_lo = jnp.exp2(jnp.where(tsub >= s, b_lo - bs, NEG))
                    acc_lo = acc_lo + jnp.sum(q_lo * ks * e_lo, axis=-1, keepdims=True) * vs
                    e_hi = jnp.exp2(b_hi - bs)
                else:
                    e_hi = jnp.exp2(jnp.where(tsub >= s - half, b_hi - bs, NEG))
                acc_hi = acc_hi + jnp.sum(q_hi * ks * e_hi, axis=-1, keepdims=True) * vs
            o = o + jnp.concatenate([acc_lo, acc_hi], axis=1).reshape(A_CH, HD)
        st_ref[h] = st * sdec[:, sl] + _bdot(vt_ref[sl, :].astype(BF16), kdec[:, sl].astype(BF16))
        outs.append(o)
    return outs


def _mlstm_gate_rows(grow_ref, gbc, triu_ref):
    nc = grow_ref.shape[0]
    g = grow_ref[...] + gbc
    srow = lax.broadcasted_iota(jnp.int32, g.shape, 1)
    g = jnp.where(srow < HEADS, g, _log_sigmoid(g)) * LOG2E
    cum = jnp.dot(g.reshape(nc * 8, LANES), triu_ref[...], precision=HI,
                  preferred_element_type=F32).reshape(nc, 8, LANES)
    b = cum[:, HEADS:, :]
    ab = jnp.concatenate([g[:, :HEADS, :] - b, b], axis=1)
    cm = ab.reshape(nc * 8, LANES)
    lane = lax.broadcasted_iota(jnp.int32, cm.shape, 1)
    d = 1
    while d < LANES:
        cm = jnp.maximum(cm, jnp.where(lane >= d, pltpu.roll(cm, d, 1), NEG))
        d *= 2
    return ab, cm.reshape(nc, 8, LANES)


def _mlstm_vectors(ab_ref, cm_ref, m_s):
    ab = ab_ref[...]
    a4, b4 = ab[:HEADS], ab[HEADS:]
    m_prev = m_s[0:HEADS, :]
    mx4 = jnp.maximum(cm_ref[0:HEADS, :], m_prev)
    mt4 = b4 + mx4
    mx_last = jnp.broadcast_to(mx4[:, C_CH - 1:C_CH], (HEADS, LANES))
    wg4 = jnp.exp2(a4 - mx_last)
    dec4 = jnp.exp2(m_prev - mx_last)
    m_s[0:HEADS, :] = jnp.broadcast_to(mt4[:, C_CH - 1:C_CH], (HEADS, LANES))
    cols = [(_colmat(mx4[h:h + 1]), _colmat(mt4[h:h + 1])) for h in range(HEADS)]
    return a4, wg4, dec4, m_prev, cols


def _mlstm_chunk(rows, per_batch, vecs):
    ri = lax.broadcasted_iota(jnp.int32, (C_CH, C_CH), 0)
    ci = lax.broadcasted_iota(jnp.int32, (C_CH, C_CH), 1)
    causal = ri >= ci
    ones_blk = jnp.ones((C_CH, HD), BF16)
    hcs = []
    for (cv_ref, co_ref, qc_s, kt_ref, ct_s), (a4, wg4, dec4, m_prev, cols) in zip(per_batch, vecs):
        for h in range(HEADS):
            sl = slice(h * HD, (h + 1) * HD)
            row = slice(h, h + 1)
            mxm, mtm = cols[h]
            ct = ct_s[h]
            qf = qc_s[rows, sl]
            kt = kt_ref[sl, :]
            vaug = jnp.concatenate([cv_ref[rows, sl], ones_blk], axis=1)

            w = jnp.exp2(jnp.where(causal, a4[row] - mxm, NEG))
            sc = _bdot(qf.astype(BF16), kt.astype(BF16))
            lhs = jnp.concatenate([(sc * w).astype(BF16),
                                   (qf * jnp.exp2(m_prev[row] - mxm)).astype(BF16)], axis=1)
            tot = _bdot(lhs, jnp.concatenate([vaug, ct.astype(BF16)], axis=0))
            hout = tot[:, :HD] / jnp.maximum(jnp.abs(tot[:, HD:]), jnp.exp2(-mtm))

            dec2 = jnp.concatenate([dec4[row], dec4[row]], axis=1)
            ct_s[h] = dec2 * ct + _bdot((kt * wg4[row]).astype(BF16), vaug)
            hcs.append(jax.nn.sigmoid(co_ref[rows, sl].astype(F32)) * hout)
    return hcs


def _rec_body(aq_ref, af_ref, ai_ref, ag_ref, cq_ref, ck_ref, cv_ref, co_ref, cg_ref, grow_ref,
              lb_ref, ang_ref, cw_ref, cb_ref, gbc_ref, cng_ref, triu_ref, mean_ref,
              ya_ref, yc_ref, st_s, xq_s, xk_s, qc_s, kt_s, vt_s, ab_s, cm_s, ct_s, m_s, lf_s, ka_s):
    bsz = aq_ref.shape[0]
    nchunk = R_BLK // C_CH

    @pl.when(pl.program_id(0) == 0)
    def _():
        st_s[...] = jnp.zeros_like(st_s)
        xq_s[:, 0:8, :] = jnp.zeros((bsz, 8, GROUP), F32)
        xk_s[:, 0:8, :] = jnp.zeros((bsz, 8, GROUP), F32)
        ct_s[...] = jnp.zeros_like(ct_s)
        m_s[...] = jnp.zeros_like(m_s)

    cw = cw_ref[...]
    cb = cb_ref[...]
    gbc = gbc_ref[...]

    def conv(x, w, bias):
        t = w[0:1] * x
        for j in range(1, CONV_K):
            t = w[j:j + 1] * x + pltpu.roll(t, 1, 0)
        return t[8:] + bias

    for bi in range(bsz):
        xq_s[bi, 8:8 + R_BLK, :] = cq_ref[bi].astype(F32)
        xk_s[bi, 8:8 + R_BLK, :] = ck_ref[bi].astype(F32)
        qc_s[bi] = _silu(conv(xq_s[bi], cw[:, :GROUP], cb[:, :GROUP]))
        kc = _silu(conv(xk_s[bi], cw[:, GROUP:], cb[:, GROUP:])) * (HD ** -0.5)
        for c in range(nchunk):
            kt_s[bi, c] = kc[c * C_CH:(c + 1) * C_CH, :].T
            vt_s[bi, c] = ai_ref[bi, c * C_CH:(c + 1) * C_CH, :].astype(F32).T
        xq_s[bi, 0:8, :] = xq_s[bi, R_BLK:R_BLK + 8, :]
        xk_s[bi, 0:8, :] = xk_s[bi, R_BLK:R_BLK + 8, :]
        ab_s[bi], cm_s[bi] = _mlstm_gate_rows(grow_ref.at[bi], gbc, triu_ref)

    lb = lb_ref[...]
    oml = 1.0 - lb
    ang = ang_ref[...]
    cng = cng_ref[...]

    def chunk(c, carry):
        rows = pl.ds(pl.multiple_of(c * C_CH, C_CH), C_CH)
        spans = [_hgrn_gates(rows, af_ref.at[bi], lb, oml, lf_s.at[bi], ka_s.at[bi]) for bi in range(bsz)]
        bounded = functools.reduce(jnp.maximum, spans) <= A_MAX_SUB_LOG2_DECAY

        def rest(is_bounded):
            vecs = [_mlstm_vectors(ab_s.at[bi, c], cm_s.at[bi, c], m_s.at[bi]) for bi in range(bsz)]
            pres = [_hgrn_prep(rows, aq_ref.at[bi], ai_ref.at[bi], ag_ref.at[bi], lf_s.at[bi], ka_s.at[bi])
                    for bi in range(bsz)]
            outs = [o for bi in range(bsz)
                    for o in _hgrn_finish(pres[bi], st_s.at[bi], vt_s.at[bi, c], is_bounded)]
            outs = _unit_rms_heads(outs, mean_ref)
            per_batch = [(cv_ref.at[bi], co_ref.at[bi], qc_s.at[bi], kt_s.at[bi, c], ct_s.at[bi])
                         for bi in range(bsz)]
            hcs = _unit_rms_heads(_mlstm_chunk(rows, per_batch, vecs), mean_ref)
            for i in range(bsz * HEADS):
                bi, h = divmod(i, HEADS)
                sl = slice(h * HD, (h + 1) * HD)
                ya_ref[bi, rows, sl] = (outs[i] * ang * pres[bi][3][:, sl]).astype(ya_ref.dtype)
                yc_ref[bi, rows, sl] = (hcs[i] * cng * _silu(cg_ref[bi, rows, sl].astype(F32))
                                        ).astype(yc_ref.dtype)

        pl.when(bounded)(functools.partial(rest, True))
        pl.when(jnp.logical_not(bounded))(functools.partial(rest, False))
        return carry

    lax.fori_loop(0, nchunk, chunk, 0)


def _recurrent(proj, grow, lb, ang, cw, cb, gbc, cng, bsz, s):
    assert A_CH == C_CH
    proj3 = proj.reshape(bsz, s, N_OUT)
    nchunk = R_BLK // C_CH
    col = lambda cid: pl.BlockSpec((bsz, R_BLK, GROUP), lambda t, cid=cid: (0, t, cid))
    full = lambda shp: pl.BlockSpec(shp, lambda t: (0,) * len(shp))
    triu = jnp.triu(jnp.ones((C_CH, C_CH), F32))
    mean_mat = jnp.full((HD, HD), 1.0 / HD, BF16)
    out = pl.BlockSpec((bsz, R_BLK, GROUP), lambda t: (0, t, 0))
    ya, yc = pl.pallas_call(
        _rec_body,
        grid=(s // R_BLK,),
        in_specs=[col(C_AQ), col(C_AF), col(C_AI), col(C_AG),
                  col(C_CQ), col(C_CK), col(C_CV), col(C_CO), col(C_CG),
                  pl.BlockSpec((bsz, nchunk, 8, LANES), lambda t: (0, t, 0, 0)),
                  full((1, GROUP)), full((1, HD)),
                  full((CONV_K, 2 * GROUP)), full((1, 2 * GROUP)),
                  full((8, 1)), full((1, HD)),
                  full((C_CH, C_CH)), full((HD, HD))],
        out_specs=[out, out],
        out_shape=[jax.ShapeDtypeStruct((bsz, s, GROUP), BF16)] * 2,
        scratch_shapes=[pltpu.VMEM((bsz, HEADS, HD, HD), F32),
                        pltpu.VMEM((bsz, R_BLK + 8, GROUP), F32), pltpu.VMEM((bsz, R_BLK + 8, GROUP), F32),
                        pltpu.VMEM((bsz, R_BLK, GROUP), F32),
                        pltpu.VMEM((bsz, nchunk, GROUP, C_CH), F32), pltpu.VMEM((bsz, nchunk, GROUP, C_CH), F32),
                        pltpu.VMEM((bsz, nchunk, 8, LANES), F32), pltpu.VMEM((bsz, nchunk, 8, LANES), F32),
                        pltpu.VMEM((bsz, HEADS, HD, 2 * HD), F32), pltpu.VMEM((bsz, 8, LANES), F32),
                        pltpu.VMEM((bsz, A_CH, GROUP), F32), pltpu.VMEM((bsz, A_CH, GROUP), F32)],
        compiler_params=_cp(("arbitrary",), 56),
        name="hgrn2_mlstm",
    )(*([proj3] * 9), grow.reshape(bsz, s // LANES, 8, LANES),
      lb, ang, cw, cb, gbc, cng, triu, mean_mat)
    return ya.reshape(bsz * s, GROUP), yc.reshape(bsz * s, GROUP)


def _flash_body(qt_ref, k_ref, vt_ref, bg_ref, lam_ref, sg_ref, y_ref,
                qz_s, m_s, l_s, acc_s, *, lam_init, online_max):
    qi = pl.program_id(2)
    qt = qt_ref[0]
    rowi = lax.broadcasted_iota(jnp.int32, (HD, BQ), 0)
    zero = jnp.zeros_like(qt)
    qz_s[0] = jnp.where(rowi < B_DQK, qt, zero)
    qz_s[1] = jnp.where(rowi >= B_DQK, qt, zero)
    m_s[...] = jnp.full_like(m_s, NEG)
    l_s[...] = jnp.zeros_like(l_s)
    acc_s[...] = jnp.zeros_like(acc_s)

    def step(kblk, vblk, q0, mask):
        qs = slice(q0, BQ)
        scores = [_bdot(kblk, qz_s[c, :, qs]) for c in range(2)]
        for c in range(2):
            s = scores[c]
            if mask is not None:
                s = jnp.where(mask, s, NEG)
            if online_max:
                m_old = m_s[c, :, qs]
                m_new = jnp.maximum(m_old, jnp.max(s, axis=0, keepdims=True))
                alpha = jnp.exp2(m_old - m_new)
                p = jnp.exp2(s - m_new)
                l_s[c, :, qs] = alpha * l_s[c, :, qs] + jnp.sum(p, axis=0, keepdims=True)
                acc_s[c, :, qs] = alpha * acc_s[c, :, qs] + _bdot(vblk, p.astype(BF16))
                m_s[c, :, qs] = m_new
            else:
                p = jnp.exp2(s)
                l_s[c, :, qs] = l_s[c, :, qs] + jnp.sum(p, axis=0, keepdims=True)
                acc_s[c, :, qs] = acc_s[c, :, qs] + _bdot(vblk, p.astype(BF16))

    def body(kb, carry):
        r0 = pl.multiple_of(kb * BK, BK)
        step(k_ref[pl.ds(r0, BK), :], vt_ref[0, kb], 0, None)
        return carry

    kpq = BQ // BK
    lax.fori_loop(0, qi * kpq, body, 0)
    for d in range(BQ // BKD):
        q0 = d * BKD
        kidx = lax.broadcasted_iota(jnp.int32, (BKD, BQ - q0), 0)
        qidx = lax.broadcasted_iota(jnp.int32, (BKD, BQ - q0), 1)
        rows = pl.ds(pl.multiple_of(qi * BQ + q0, BKD), BKD)
        lo = q0 % BK
        vblk = vt_ref[0, qi * kpq + q0 // BK][:, lo:lo + BKD]
        step(k_ref[rows, :], vblk, q0, kidx <= qidx)

    lp = lam_ref[...]
    lam = (jnp.exp(jnp.sum(lp[0:1] * lp[1:2], axis=1, keepdims=True))
           - jnp.exp(jnp.sum(lp[2:3] * lp[3:4], axis=1, keepdims=True)) + lam_init)
    ot = acc_s[0] / l_s[0] - lam * (acc_s[1] / l_s[1])
    o = ot.T
    y = _rms(o, sg_ref[...]) * (1.0 - lam_init) * _silu(bg_ref[...].astype(F32))
    y_ref[...] = y.astype(y_ref.dtype)


def _flash_call(qt, kk, vt, proj, lam_p, sg, *, bsz, s, lam_init, online_max):
    nq = s // BQ
    return pl.pallas_call(
        functools.partial(_flash_body, lam_init=lam_init, online_max=online_max),
        grid=(bsz, HEADS, nq),
        in_specs=[pl.BlockSpec((1, HD, BQ), lambda b, h, q: (b, h, q)),
                  pl.BlockSpec((s, HD), lambda b, h, q: (b, h)),
                  pl.BlockSpec((1, s // BK, HD, BK), lambda b, h, q: (b, 0, h, 0)),
                  pl.BlockSpec((BQ, HD), lambda b, h, q: (b * nq + q, C_BG * HEADS + h)),
                  pl.BlockSpec((4, B_DQK), lambda b, h, q: (0, 0)),
                  pl.BlockSpec((1, HD), lambda b, h, q: (0, 0))],
        out_specs=pl.BlockSpec((BQ, HD), lambda b, h, q: (b * nq + q, h)),
        out_shape=jax.ShapeDtypeStruct((bsz * s, GROUP), BF16),
        scratch_shapes=[pltpu.VMEM((2, HD, BQ), BF16), pltpu.VMEM((2, 1, BQ), F32),
                        pltpu.VMEM((2, 1, BQ), F32), pltpu.VMEM((2, HD, BQ), F32)],
        compiler_params=_cp(("arbitrary", "arbitrary", "arbitrary"), 48),
        name="diff_flash_online" if online_max else "diff_flash",
    )(qt, kk, vt, proj, lam_p, sg)


def _flash(qt, kk, vt, proj, lam_p, sg, qk_g, bsz, s, lam_init):
    bound = (B_DQK ** 0.5) * LOG2E * jnp.max(jnp.abs(qk_g[0])) * jnp.max(jnp.abs(qk_g[1]))
    args = (qt, kk, vt, proj, lam_p, sg)
    call = functools.partial(_flash_call, bsz=bsz, s=s, lam_init=lam_init)
    return lax.cond(bound <= MAX_UNSHIFTED_LOG2_SCORE - math.log2(s),
                    functools.partial(call, online_max=False),
                    functools.partial(call, online_max=True), *args)


def _memkv_body(mem_ref, g_ref, w_ref, kg_ref, km_ref, vm_ref):
    mn = _rms(mem_ref[0], g_ref[...]).astype(BF16)
    kv = _bdot(mn, w_ref[...])
    for h in range(HEADS):
        sl = slice(h * HD, (h + 1) * HD)
        km_ref[0, :, sl] = _rms(kv[:, sl], kg_ref[...]).astype(BF16)
    vm_ref[0] = kv[:, GROUP:].astype(BF16)


def _memkv(mem, g, w, kg):
    bsz = mem.shape[0]
    return pl.pallas_call(
        _memkv_body,
        grid=(bsz,),
        in_specs=[pl.BlockSpec((1, N_MEM, D_MODEL), lambda b: (b, 0, 0)),
                  pl.BlockSpec((1, D_MODEL), lambda b: (0, 0)),
                  pl.BlockSpec((D_MODEL, 2 * GROUP), lambda b: (0, 0)),
                  pl.BlockSpec((1, HD), lambda b: (0, 0))],
        out_specs=[pl.BlockSpec((1, N_MEM, GROUP), lambda b: (b, 0, 0))] * 2,
        out_shape=[jax.ShapeDtypeStruct((bsz, N_MEM, GROUP), BF16)] * 2,
        compiler_params=_cp(("arbitrary",), 32),
        name="mem_kv",
    )(mem, g, w, kg)


def _merge_body(x_ref, ya_ref, yb_ref, yc_ref, xq_ref, xg_ref, km_ref, vm_ref, qg_ref, w_ref, o_ref):
    acc = x_ref[...]
    acc = acc + _bdot(ya_ref[...], w_ref[0:GROUP, :])
    acc = acc + _bdot(yb_ref[...], w_ref[GROUP:2 * GROUP, :])
    acc = acc + _bdot(yc_ref[...], w_ref[2 * GROUP:3 * GROUP, :])
    qg = qg_ref[...]
    yx = []
    for h in range(HEADS):
        sl = slice(h * HD, (h + 1) * HD)
        qn = (_rms(xq_ref[:, sl].astype(F32), qg) * (HD ** -0.5)).astype(BF16)
        s = lax.dot_general(qn, km_ref[0, :, sl], NT, preferred_element_type=F32)
        p = jnp.exp(s - jnp.max(s, axis=-1, keepdims=True))
        l = jnp.sum(p, axis=-1, keepdims=True)
        ox = _bdot(p.astype(BF16), vm_ref[0, :, sl]) / l
        yx.append((ox * _silu(xg_ref[:, sl].astype(F32))).astype(BF16))
    o_ref[...] = acc + _bdot(jnp.concatenate(yx, axis=1), w_ref[3 * GROUP:4 * GROUP, :])


def _merge(x2, ya, yb, yc, proj, km, vm, qg, w_out, bsz, s):
    nb = s // M_BM
    t = bsz * s
    blk = lambda: pl.BlockSpec((M_BM, GROUP), lambda i: (i, 0))
    col = lambda cid: pl.BlockSpec((M_BM, GROUP), lambda i, cid=cid: (i, cid))
    return pl.pallas_call(
        _merge_body,
        grid=(t // M_BM,),
        in_specs=[pl.BlockSpec((M_BM, D_MODEL), lambda i: (i, 0)),
                  blk(), blk(), blk(), col(C_XQ), col(C_XG),
                  pl.BlockSpec((1, N_MEM, GROUP), lambda i: (i // nb, 0, 0)),
                  pl.BlockSpec((1, N_MEM, GROUP), lambda i: (i // nb, 0, 0)),
                  pl.BlockSpec((1, HD), lambda i: (0, 0)),
                  pl.BlockSpec((4 * GROUP, D_MODEL), lambda i: (0, 0))],
        out_specs=pl.BlockSpec((M_BM, D_MODEL), lambda i: (i, 0)),
        out_shape=jax.ShapeDtypeStruct((t, D_MODEL), F32),
        compiler_params=_cp(("arbitrary",), 48),
        name="xattn_out_proj",
    )(x2, ya, yb, yc, proj, proj, km, vm, qg, w_out)


def _layer(x2, mem, cosm, sinm, layer_idx, lb, norm_g, w_in, mlstm_gate_b, hgrn_norm_g,
           diff_qk_norm_g, diff_lambda, diff_subln_g, mlstm_conv_w, mlstm_conv_b,
           mlstm_norm_g, mem_norm_g, w_mem_kv, xattn_qk_norm_g, w_out, bsz, s):
    g12 = 12 * GROUP
    w_main = jnp.concatenate([w_in[:, :g12], w_in[:, g12 + 2 * HEADS:]], axis=1).astype(BF16)
    wgt = jnp.pad(w_in[:, g12:g12 + 2 * HEADS].T, ((0, 8), (0, 0))).astype(BF16)
    gq = jnp.tile(diff_qk_norm_g[0], GROUP // B_DQK)[None, :]
    gk = jnp.tile(diff_qk_norm_g[1], GROUP // B_DQK)[None, :]
    proj, grow, qt, kk, vt = _inproj(x2, norm_g[None, :], w_main, wgt, cosm, sinm, gq, gk, bsz, s)

    ya, yc = _recurrent(proj, grow, lb[None, :], hgrn_norm_g[None, :], mlstm_conv_w,
                        mlstm_conv_b[None, :], mlstm_gate_b[:, None], mlstm_norm_g[None, :], bsz, s)

    lam_init = 0.8 - 0.6 * math.exp(-0.3 * layer_idx)
    yb = _flash(qt, kk, vt, proj, diff_lambda, diff_subln_g[None, :], diff_qk_norm_g, bsz, s, lam_init)

    km, vm = _memkv(mem, mem_norm_g[None, :], w_mem_kv.astype(BF16), xattn_qk_norm_g[1][None, :])
    return _merge(x2, ya, yb, yc, proj, km, vm, xattn_qk_norm_g[0][None, :],
                  w_out.astype(BF16), bsz, s)


def kernel(x, mem, positions, norm_g, w_in, mlstm_gate_b, hgrn_lb_logits, hgrn_norm_g,
           diff_qk_norm_g, diff_lambda, diff_subln_g, mlstm_conv_w, mlstm_conv_b,
           mlstm_norm_g, mem_norm_g, w_mem_kv, xattn_qk_norm_g, w_out):
    bsz, s, d = x.shape
    depth = norm_g.shape[0]
    assert d == D_MODEL and mem.shape[1] == N_MEM
    assert all(s % blk == 0 for blk in (IN_BM, R_BLK, P_BM, BQ, BK, M_BM)) and (bsz * s) % T_BM == 0
    cosm, sinm = _rope_tables(positions)
    sm = jax.nn.softmax(hgrn_lb_logits.astype(F32), axis=0)
    lower_bounds = jnp.cumsum(sm, axis=0) - sm[0]
    x2 = x.reshape(bsz * s, d)
    for l in range(depth):
        x2 = _layer(x2, mem, cosm, sinm, l, lower_bounds[l], norm_g[l], w_in[l], mlstm_gate_b[l],
                    hgrn_norm_g[l], diff_qk_norm_g[l], diff_lambda[l], diff_subln_g[l],
                    mlstm_conv_w[l], mlstm_conv_b[l], mlstm_norm_g[l], mem_norm_g[l],
                    w_mem_kv[l], xattn_qk_norm_g[l], w_out[l], bsz, s)
    return x2.reshape(bsz, s, d)
```

```python
import functools
import math

import jax
import jax.numpy as jnp
from jax import lax
from jax.experimental import pallas as pl
from jax.experimental.pallas import tpu as pltpu

F32 = jnp.float32
BF16 = jnp.bfloat16
HI = lax.Precision.HIGHEST

D_MODEL = 1024
N_MEM = 256
GROUP = 512
HEADS = 4
HD = 128
B_DQK = 64
ROPE_DIM = 16
ROPE_THETA = 500000.0
CONV_K = 4
EPS = 1e-6
NEG = -1e30
TINY = 1e-30
LOG2E = math.log2(math.e)
MAX_UNSHIFTED_LOG2_SCORE = 100.0

LANES = 128
N_MAIN = 15 * GROUP

(W_AQ, W_AF, W_AI, W_AG, W_BQ, W_BK, W_BV, W_BG,
 W_CQ, W_CK, W_CV, W_CO, W_CG, W_XQ, W_XG) = range(15)
(C_AQ, C_AF, C_AI, C_AG, C_BG, C_CQ, C_CV, C_CO, C_CG, C_XQ, C_XG) = range(11)
OUT_COL = {W_AQ: C_AQ, W_AF: C_AF, W_AI: C_AI, W_AG: C_AG, W_BG: C_BG, W_CQ: C_CQ,
           W_CV: C_CV, W_CO: C_CO, W_CG: C_CG, W_XQ: C_XQ, W_XG: C_XG}
N_OUT = len(OUT_COL) * GROUP

IN_BM, IN_BN = 512, 1536
R_BLK = 512
A_CH, A_SUB = 128, 16
A_MAX_SUB_LOG2_DECAY = 60.0
C_CH = 128
P_BM = 512
BQ = 2048
BK = 1024
BKD = 512
M_BM = 512
T_BM = 1024

NT = (((1,), (1,)), ((), ()))
TN = (((0,), (0,)), ((), ()))


def _cp(sem, vmem_mib):
    return pltpu.CompilerParams(dimension_semantics=sem, vmem_limit_bytes=vmem_mib * 2 ** 20)


def _silu(x):
    return x * jax.nn.sigmoid(x)


def _log_sigmoid(x):
    return jnp.minimum(x, 0.0) - jnp.log1p(jnp.exp(-jnp.abs(x)))


def _rms(x, g):
    return x * lax.rsqrt(jnp.mean(x * x, axis=-1, keepdims=True) + EPS) * g


def _bdot(a, b):
    return jnp.dot(a, b, preferred_element_type=F32)


def _unit_rms_heads(xs, mean_ref):
    n = xs[0].shape[0]
    x = jnp.concatenate(xs, axis=0)
    y = x * lax.rsqrt(_bdot((x * x).astype(BF16), mean_ref[...]) + EPS)
    return [y[i * n:(i + 1) * n] for i in range(len(xs))]


def _cumsum_rows(x):
    row = lax.broadcasted_iota(jnp.int32, x.shape, 0)
    d = 1
    while d < x.shape[0]:
        x = x + jnp.where(row >= d, pltpu.roll(x, d, 0), 0.0)
        d *= 2
    return x


def _colmat(r):
    return jnp.broadcast_to(r, (r.shape[1], r.shape[1])).T


def _rope_body(pos_ref, invf_ref, cos_ref, sin_ref):
    ang = pos_ref[...].astype(F32) * invf_ref[...]
    lane = lax.broadcasted_iota(jnp.int32, ang.shape, 1) % B_DQK
    c, s = jnp.cos(ang), jnp.sin(ang)
    cos_ref[...] = jnp.where(lane < ROPE_DIM, c, 1.0)
    sin_ref[...] = jnp.where(lane < ROPE_DIM // 2, -s, jnp.where(lane < ROPE_DIM, s, 0.0))


def _rope_tables(positions):
    t = positions.size
    inv_freq = ROPE_THETA ** (-jnp.arange(0, ROPE_DIM, 2, dtype=F32) / ROPE_DIM)
    invf = jnp.tile(inv_freq, LANES // (ROPE_DIM // 2))[None, :]
    return pl.pallas_call(
        _rope_body,
        grid=(t // T_BM,),
        in_specs=[pl.BlockSpec((T_BM, 1), lambda i: (i, 0)),
                  pl.BlockSpec((1, LANES), lambda i: (0, 0))],
        out_specs=[pl.BlockSpec((T_BM, LANES), lambda i: (i, 0))] * 2,
        out_shape=[jax.ShapeDtypeStruct((t, LANES), F32)] * 2,
        compiler_params=_cp(("arbitrary",), 32),
        name="rope_tables",
    )(positions.reshape(t, 1), invf)


def _inproj_body(x_ref, g_ref, w_ref, wgt_ref, cos_ref, sin_ref, gq_ref, gk_ref, bd_ref, cw_ref, cb_ref,
                 o_ref, gr_ref, qt_ref, k_ref, vt_ref, ckt_ref, avt_ref, tmp_ref, carry_ref,
                 *, steps_per_seq):
    h = _rms(x_ref[...], g_ref[...]).astype(BF16)
    gr = lax.dot_general(wgt_ref[...], h, NT, preferred_element_type=F32)
    for j in range(IN_BM // LANES):
        gr_ref[j] = gr[:8, j * LANES:(j + 1) * LANES]

    reps = GROUP // LANES
    cosm = jnp.concatenate([cos_ref[...]] * reps, axis=1)
    sinm = jnp.concatenate([sin_ref[...]] * reps, axis=1)
    lane = lax.broadcasted_iota(jnp.int32, (IN_BM, GROUP), 1)
    first_half = (lane % ROPE_DIM) < ROPE_DIM // 2

    def norm_rope(x, gain):
        ms = _bdot((x * x).astype(BF16), bd_ref[...])
        xn = x * lax.rsqrt(ms + EPS) * gain
        partner = jnp.where(first_half,
                            pltpu.roll(xn, GROUP - ROPE_DIM // 2, 1),
                            pltpu.roll(xn, ROPE_DIM // 2, 1))
        return xn * cosm + partner * sinm

    seq_start = pl.program_id(0) % steps_per_seq == 0

    def conv_silu(y, g):
        prev = jnp.where(seq_start, 0.0, carry_ref[g])
        carry_ref[g] = y[IN_BM - 8:, :]
        x = jnp.concatenate([prev, y], axis=0)
        w = cw_ref[:, g * GROUP:(g + 1) * GROUP]
        t = w[0:1] * x
        for j in range(1, CONV_K):
            t = w[j:j + 1] * x + pltpu.roll(t, 1, 0)
        return _silu(t[8:] + cb_ref[:, g * GROUP:(g + 1) * GROUP])

    def store_chunks_transposed(dst_ref, val):
        tmp_ref[...] = val
        for c in range(IN_BM // C_CH):
            dst_ref[0, c] = tmp_ref[c * C_CH:(c + 1) * C_CH, :].T.astype(BF16)

    def project(wid):
        return _bdot(h, w_ref[:, wid * GROUP:(wid + 1) * GROUP])

    def finish(wid, y):
        if wid == W_BQ:
            tmp_ref[...] = norm_rope(y, gq_ref[...]) * (B_DQK ** -0.5 * LOG2E)
            qt_ref[0] = tmp_ref[...].T.astype(BF16)
        elif wid == W_BK:
            k_ref[...] = norm_rope(y, gk_ref[...]).astype(BF16)
        elif wid == W_BV:
            tmp_ref[...] = y
            vt_ref[0, 0] = tmp_ref[...].T.astype(BF16)
        elif wid == W_CK:
            store_chunks_transposed(ckt_ref, conv_silu(y, 1) * (HD ** -0.5))
        else:
            if wid == W_CQ:
                y = conv_silu(y, 0)
            elif wid == W_AI:
                store_chunks_transposed(avt_ref, y)
            cid = OUT_COL[wid]
            o_ref[:, cid * GROUP:(cid + 1) * GROUP] = y.astype(o_ref.dtype)

    busy = (W_BQ, W_BK, W_CQ, W_CK, W_BV, W_AI)
    plain = [w for w in range(N_MAIN // GROUP) if w not in busy]
    ys = [project(wid) for wid in busy]
    for i, wid in enumerate(busy):
        finish(wid, ys[i])
        for p in plain[i * len(plain) // len(busy):(i + 1) * len(plain) // len(busy)]:
            finish(p, project(p))


def _inproj(x2, g, w_main, wgt, cosm, sinm, gq, gk, cw, cb, bsz, s):
    assert IN_BM == P_BM and IN_BM % C_CH == 0
    t = x2.shape[0]
    nb = s // P_BM
    ppk = BK // P_BM
    cpb = IN_BM // C_CH
    seg = jnp.arange(GROUP) // B_DQK
    bd = (jnp.where(seg[:, None] == seg[None, :], 1.0 / B_DQK, 0.0)).astype(BF16)
    chunk_t = pl.BlockSpec((1, cpb, GROUP, C_CH), lambda i: (i // nb, i % nb, 0, 0))
    return pl.pallas_call(
        functools.partial(_inproj_body, steps_per_seq=nb),
        grid=(t // IN_BM,),
        in_specs=[pl.BlockSpec((IN_BM, D_MODEL), lambda i: (i, 0)),
                  pl.BlockSpec((1, D_MODEL), lambda i: (0, 0)),
                  pl.BlockSpec((D_MODEL, N_MAIN), lambda i: (0, 0), pipeline_mode=pl.Buffered(1)),
                  pl.BlockSpec((16, D_MODEL), lambda i: (0, 0)),
                  pl.BlockSpec((IN_BM, LANES), lambda i: (i, 0)),
                  pl.BlockSpec((IN_BM, LANES), lambda i: (i, 0)),
                  pl.BlockSpec((1, GROUP), lambda i: (0, 0)),
                  pl.BlockSpec((1, GROUP), lambda i: (0, 0)),
                  pl.BlockSpec((GROUP, GROUP), lambda i: (0, 0)),
                  pl.BlockSpec((CONV_K, 2 * GROUP), lambda i: (0, 0)),
                  pl.BlockSpec((1, 2 * GROUP), lambda i: (0, 0))],
        out_specs=[pl.BlockSpec((IN_BM, N_OUT), lambda i: (i, 0)),
                   pl.BlockSpec((IN_BM // LANES, 8, LANES), lambda i: (i, 0, 0)),
                   pl.BlockSpec((1, GROUP, P_BM), lambda i: (i // nb, 0, i % nb)),
                   pl.BlockSpec((P_BM, GROUP), lambda i: (i, 0)),
                   pl.BlockSpec((1, 1, GROUP, P_BM),
                                lambda i: (i // nb, (i % nb) // ppk, 0, (i % nb) % ppk)),
                   chunk_t, chunk_t],
        out_shape=[jax.ShapeDtypeStruct((t, N_OUT), BF16),
                   jax.ShapeDtypeStruct((t // LANES, 8, LANES), F32),
                   jax.ShapeDtypeStruct((bsz, GROUP, s), BF16),
                   jax.ShapeDtypeStruct((t, GROUP), BF16),
                   jax.ShapeDtypeStruct((bsz, s // BK, GROUP, BK), BF16),
                   jax.ShapeDtypeStruct((bsz, s // C_CH, GROUP, C_CH), BF16),
                   jax.ShapeDtypeStruct((bsz, s // C_CH, GROUP, C_CH), BF16)],
        scratch_shapes=[pltpu.VMEM((IN_BM, GROUP), F32),
                        pltpu.VMEM((2, 8, GROUP), F32)],
        compiler_params=_cp(("arbitrary",), 56),
        name="in_proj",
    )(x2, g, w_main, wgt, cosm, sinm, gq, gk, bd, cw, cb)


def _hgrn_gates(rows, af_ref, lb, oml, lf_s, ka_s):
    sg = jax.nn.sigmoid(af_ref[rows, :].astype(F32))
    logf = jnp.log2(jnp.maximum(lb + oml * sg, TINY))
    lf_s[...] = logf
    ka_s[...] = oml * (1.0 - sg)
    sums = jnp.sum(logf.reshape(A_CH // A_SUB, A_SUB, GROUP), axis=1)
    return -jnp.min(sums)


def _hgrn_prep(rows, aq_ref, ai_ref, ag_ref, lf_s, ka_s):
    q = _silu(aq_ref[rows, :].astype(F32)) * (HD ** -0.5)
    v = ai_ref[rows, :].astype(F32)
    gate = _silu(ag_ref[rows, :].astype(F32))
    return q, ka_s[...], v, gate, _cumsum_rows(lf_s[...])


def _hgrn_finish(pre, st_ref, vt_ref, bounded):
    q, ka, v, gate, b = pre
    nsub = A_CH // A_SUB
    half = A_SUB // 2
    row = lax.broadcasted_iota(jnp.int32, (A_CH, HD), 0)
    tsub = lax.broadcasted_iota(jnp.int32, (nsub, half, HD), 1)
    ti = lax.broadcasted_iota(jnp.int32, (A_CH, A_CH), 0)
    si = lax.broadcasted_iota(jnp.int32, (A_CH, A_CH), 1)
    blast = b[A_CH - 1:A_CH, :]
    qdec = q * jnp.exp2(b)
    kdec = ka * jnp.exp2(blast - b)
    sdec = jnp.exp2(blast)
    outs = []
    for h in range(HEADS):
        sl = slice(h * HD, (h + 1) * HD)
        qh, kh, vh, bh = q[:, sl], ka[:, sl], v[:, sl], b[:, sl]
        vhb = vh.astype(BF16)
        st = st_ref[h]
        o = lax.dot_general(qdec[:, sl].astype(BF16), st.astype(BF16), NT, preferred_element_type=F32)
        parts = []
        for i in range(nsub):
            lo = i * A_SUB
            hi = lo + A_SUB if bounded else lo
            if hi == 0:
                parts.append(jnp.zeros((A_SUB, A_CH), F32))
                continue
            bi = bh[lo - 1:lo, :] if i else jnp.zeros((1, HD), F32)
            qi = (qh[lo:lo + A_SUB] * jnp.exp2(bh[lo:lo + A_SUB] - bi)).astype(BF16)
            kp = (kh[:hi] * jnp.exp2(bi - bh[:hi])).astype(BF16)
            part = lax.dot_general(qi, kp, NT, preferred_element_type=F32)
            if hi < A_CH:
                part = jnp.concatenate([part, jnp.zeros((A_SUB, A_CH - hi), F32)], axis=1)
            parts.append(part)
        amat = jnp.concatenate(parts, axis=0)
        if bounded:
            amat = jnp.where(si <= ti, amat, 0.0)
        o = o + _bdot(amat.astype(BF16), vhb)
        if not bounded:
            q3 = qh.reshape(nsub, A_SUB, HD)
            k3 = kh.reshape(nsub, A_SUB, HD)
            b3 = bh.reshape(nsub, A_SUB, HD)
            v3 = vh.reshape(nsub, A_SUB, HD)
            q_lo, q_hi = q3[:, :half], q3[:, half:]
            b_lo, b_hi = b3[:, :half], b3[:, half:]
            acc_lo = jnp.zeros((nsub, half, HD), F32)
            acc_hi = jnp.zeros((nsub, half, HD), F32)
            for s in range(A_SUB):
                bs, ks, vs = b3[:, s:s + 1, :], k3[:, s:s + 1, :], v3[:, s:s + 1, :]
                if s < half:
                    e_lo = jnp.exp2(jnp.where(tsub >= s, b_lo - bs, NEG))
                    acc_lo = acc_lo + jnp.sum(q_lo * ks * e_lo, axis=-1, keepdims=True) * vs
                    e_hi = jnp.exp2(b_hi - bs)
                else:
                    e_hi = jnp.exp2(jnp.where(tsub >= s - half, b_hi - bs, NEG))
                acc_hi = acc_hi + jnp.sum(q_hi * ks * e_hi, axis=-1, keepdims=True) * vs
            o = o + jnp.concatenate([acc_lo, acc_hi], axis=1).reshape(A_CH, HD)
        st_ref[h] = st * sdec[:, sl] + _bdot(vt_ref[sl, :], kdec[:, sl].astype(BF16))
        outs.append(o)
    return outs


def _mlstm_gate_rows(grow_ref, gbc, triu_ref):
    nc = grow_ref.shape[0]
    g = grow_ref[...] + gbc
    srow = lax.broadcasted_iota(jnp.int32, g.shape, 1)
    g = jnp.where(srow < HEADS, g, _log_sigmoid(g)) * LOG2E
    cum = jnp.dot(g.reshape(nc * 8, LANES), triu_ref[...], precision=HI,
                  preferred_element_type=F32).reshape(nc, 8, LANES)
    b = cum[:, HEADS:, :]
    ab = jnp.concatenate([g[:, :HEADS, :] - b, b], axis=1)
    cm = ab.reshape(nc * 8, LANES)
    lane = lax.broadcasted_iota(jnp.int32, cm.shape, 1)
    d = 1
    while d < LANES:
        cm = jnp.maximum(cm, jnp.where(lane >= d, pltpu.roll(cm, d, 1), NEG))
        d *= 2
    return ab, cm.reshape(nc, 8, LANES)


def _mlstm_vectors(ab_ref, cm_ref, m_s):
    ab = ab_ref[...]
    a4, b4 = ab[:HEADS], ab[HEADS:]
    m_prev = m_s[0:HEADS, :]
    mx4 = jnp.maximum(cm_ref[0:HEADS, :], m_prev)
    mt4 = b4 + mx4
    mx_last = jnp.broadcast_to(mx4[:, C_CH - 1:C_CH], (HEADS, LANES))
    wg4 = jnp.exp2(a4 - mx_last)
    dec4 = jnp.exp2(m_prev - mx_last)
    m_s[0:HEADS, :] = jnp.broadcast_to(mt4[:, C_CH - 1:C_CH], (HEADS, LANES))
    cols = [(_colmat(mx4[h:h + 1]), _colmat(mt4[h:h + 1])) for h in range(HEADS)]
    return a4, wg4, dec4, m_prev, cols


def _mlstm_chunk(rows, per_batch, vecs):
    ri = lax.broadcasted_iota(jnp.int32, (C_CH, C_CH), 0)
    ci = lax.broadcasted_iota(jnp.int32, (C_CH, C_CH), 1)
    causal = ri >= ci
    ones_blk = jnp.ones((C_CH, HD), BF16)
    hcs = []
    for (cv_ref, co_ref, cq_ref, kt_ref, ct_s), (a4, wg4, dec4, m_prev, cols) in zip(per_batch, vecs):
        for h in range(HEADS):
            sl = slice(h * HD, (h + 1) * HD)
            row = slice(h, h + 1)
            mxm, mtm = cols[h]
            ct = ct_s[h]
            qb = cq_ref[rows, sl]
            qf = qb.astype(F32)
            ktb = kt_ref[sl, :]
            kt = ktb.astype(F32)
            vaug = jnp.concatenate([cv_ref[rows, sl], ones_blk], axis=1)

            w = jnp.exp2(jnp.where(causal, a4[row] - mxm, NEG))
            sc = _bdot(qb, ktb)
            lhs = jnp.concatenate([(sc * w).astype(BF16),
                                   (qf * jnp.exp2(m_prev[row] - mxm)).astype(BF16)], axis=1)
            tot = _bdot(lhs, jnp.concatenate([vaug, ct.astype(BF16)], axis=0))
            hout = tot[:, :HD] / jnp.maximum(jnp.abs(tot[:, HD:]), jnp.exp2(-mtm))

            dec2 = jnp.concatenate([dec4[row], dec4[row]], axis=1)
            ct_s[h] = dec2 * ct + _bdot((kt * wg4[row]).astype(BF16), vaug)
            hcs.append(jax.nn.sigmoid(co_ref[rows, sl].astype(F32)) * hout)
    return hcs


def _rec_body(aq_ref, af_ref, ai_ref, ag_ref, cq_ref, cv_ref, co_ref, cg_ref, kt_ref, vt_ref, grow_ref,
              lb_ref, ang_ref, gbc_ref, cng_ref, triu_ref, mean_ref,
              ya_ref, yc_ref, st_s, ab_s, cm_s, ct_s, m_s, lf_s, ka_s):
    bsz = aq_ref.shape[0]
    nchunk = R_BLK // C_CH

    @pl.when(pl.program_id(0) == 0)
    def _():
        st_s[...] = jnp.zeros_like(st_s)
        ct_s[...] = jnp.zeros_like(ct_s)
        m_s[...] = jnp.zeros_like(m_s)

    gbc = gbc_ref[...]
    for bi in range(bsz):
        ab_s[bi], cm_s[bi] = _mlstm_gate_rows(grow_ref.at[bi], gbc, triu_ref)

    lb = lb_ref[...]
    oml = 1.0 - lb
    ang = ang_ref[...]
    cng = cng_ref[...]

    def chunk(c, carry):
        rows = pl.ds(pl.multiple_of(c * C_CH, C_CH), C_CH)
        spans = [_hgrn_gates(rows, af_ref.at[bi], lb, oml, lf_s.at[bi], ka_s.at[bi]) for bi in range(bsz)]
        bounded = functools.reduce(jnp.maximum, spans) <= A_MAX_SUB_LOG2_DECAY

        def rest(is_bounded):
            vecs = [_mlstm_vectors(ab_s.at[bi, c], cm_s.at[bi, c], m_s.at[bi]) for bi in range(bsz)]
            pres = [_hgrn_prep(rows, aq_ref.at[bi], ai_ref.at[bi], ag_ref.at[bi], lf_s.at[bi], ka_s.at[bi])
                    for bi in range(bsz)]
            outs = [o for bi in range(bsz)
                    for o in _hgrn_finish(pres[bi], st_s.at[bi], vt_ref.at[bi, c], is_bounded)]
            outs = _unit_rms_heads(outs, mean_ref)
            per_batch = [(cv_ref.at[bi], co_ref.at[bi], cq_ref.at[bi], kt_ref.at[bi, c], ct_s.at[bi])
                         for bi in range(bsz)]
            hcs = _unit_rms_heads(_mlstm_chunk(rows, per_batch, vecs), mean_ref)
            for i in range(bsz * HEADS):
                bi, h = divmod(i, HEADS)
                sl = slice(h * HD, (h + 1) * HD)
                ya_ref[bi, rows, sl] = (outs[i] * ang * pres[bi][3][:, sl]).astype(ya_ref.dtype)
                yc_ref[bi, rows, sl] = (hcs[i] * cng * _silu(cg_ref[bi, rows, sl].astype(F32))
                                        ).astype(yc_ref.dtype)

        pl.when(bounded)(functools.partial(rest, True))
        pl.when(jnp.logical_not(bounded))(functools.partial(rest, False))
        return carry

    lax.fori_loop(0, nchunk, chunk, 0)


def _recurrent(proj, ckt, avt, grow, lb, ang, gbc, cng, bsz, s):
    assert A_CH == C_CH
    proj3 = proj.reshape(bsz, s, N_OUT)
    nchunk = R_BLK // C_CH
    col = lambda cid: pl.BlockSpec((bsz, R_BLK, GROUP), lambda t, cid=cid: (0, t, cid))
    full = lambda shp: pl.BlockSpec(shp, lambda t: (0,) * len(shp))
    chunk_t = pl.BlockSpec((bsz, nchunk, GROUP, C_CH), lambda t: (0, t, 0, 0))
    triu = jnp.triu(jnp.ones((C_CH, C_CH), F32))
    mean_mat = jnp.full((HD, HD), 1.0 / HD, BF16)
    out = pl.BlockSpec((bsz, R_BLK, GROUP), lambda t: (0, t, 0))
    ya, yc = pl.pallas_call(
        _rec_body,
        grid=(s // R_BLK,),
        in_specs=[col(C_AQ), col(C_AF), col(C_AI), col(C_AG),
                  col(C_CQ), col(C_CV), col(C_CO), col(C_CG), chunk_t, chunk_t,
                  pl.BlockSpec((bsz, nchunk, 8, LANES), lambda t: (0, t, 0, 0)),
                  full((1, GROUP)), full((1, HD)),
                  full((8, 1)), full((1, HD)),
                  full((C_CH, C_CH)), full((HD, HD))],
        out_specs=[out, out],
        out_shape=[jax.ShapeDtypeStruct((bsz, s, GROUP), BF16)] * 2,
        scratch_shapes=[pltpu.VMEM((bsz, HEADS, HD, HD), F32),
                        pltpu.VMEM((bsz, nchunk, 8, LANES), F32), pltpu.VMEM((bsz, nchunk, 8, LANES), F32),
                        pltpu.VMEM((bsz, HEADS, HD, 2 * HD), F32), pltpu.VMEM((bsz, 8, LANES), F32),
                        pltpu.VMEM((bsz, A_CH, GROUP), F32), pltpu.VMEM((bsz, A_CH, GROUP), F32)],
        compiler_params=_cp(("arbitrary",), 48),
        name="hgrn2_mlstm",
    )(*([proj3] * 8), ckt, avt, grow.reshape(bsz, s // LANES, 8, LANES),
      lb, ang, gbc, cng, triu, mean_mat)
    return ya.reshape(bsz * s, GROUP), yc.reshape(bsz * s, GROUP)


def _flash_body(qt_ref, k_ref, vt_ref, bg_ref, lam_ref, sg_ref, y_ref,
                qz_s, m_s, l_s, acc_s, *, lam_init, online_max):
    qi = pl.program_id(2)
    qt = qt_ref[0]
    rowi = lax.broadcasted_iota(jnp.int32, (HD, BQ), 0)
    zero = jnp.zeros_like(qt)
    qz_s[0] = jnp.where(rowi < B_DQK, qt, zero)
    qz_s[1] = jnp.where(rowi >= B_DQK, qt, zero)
    m_s[...] = jnp.full_like(m_s, NEG)
    l_s[...] = jnp.zeros_like(l_s)
    acc_s[...] = jnp.zeros_like(acc_s)

    def step(kblk, vblk, q0, mask):
        qs = slice(q0, BQ)
        scores = [_bdot(kblk, qz_s[c, :, qs]) for c in range(2)]
        for c in range(2):
            s = scores[c]
            if mask is not None:
                s = jnp.where(mask, s, NEG)
            if online_max:
                m_old = m_s[c, :, qs]
                m_new = jnp.maximum(m_old, jnp.max(s, axis=0, keepdims=True))
                alpha = jnp.exp2(m_old - m_new)
                p = jnp.exp2(s - m_new)
                l_s[c, :, qs] = alpha * l_s[c, :, qs] + jnp.sum(p, axis=0, keepdims=True)
                acc_s[c, :, qs] = alpha * acc_s[c, :, qs] + _bdot(vblk, p.astype(BF16))
                m_s[c, :, qs] = m_new
            else:
                p = jnp.exp2(s)
                l_s[c, :, qs] = l_s[c, :, qs] + jnp.sum(p, axis=0, keepdims=True)
                acc_s[c, :, qs] = acc_s[c, :, qs] + _bdot(vblk, p.astype(BF16))

    def body(kb, carry):
        r0 = pl.multiple_of(kb * BK, BK)
        step(k_ref[pl.ds(r0, BK), :], vt_ref[0, kb], 0, None)
        return carry

    kpq = BQ // BK
    lax.fori_loop(0, qi * kpq, body, 0)
    for d in range(BQ // BKD):
        q0 = d * BKD
        kidx = lax.broadcasted_iota(jnp.int32, (BKD, BQ - q0), 0)
        qidx = lax.broadcasted_iota(jnp.int32, (BKD, BQ - q0), 1)
        rows = pl.ds(pl.multiple_of(qi * BQ + q0, BKD), BKD)
        lo = q0 % BK
        vblk = vt_ref[0, qi * kpq + q0 // BK][:, lo:lo + BKD]
        step(k_ref[rows, :], vblk, q0, kidx <= qidx)

    lp = lam_ref[...]
    lam = (jnp.exp(jnp.sum(lp[0:1] * lp[1:2], axis=1, keepdims=True))
           - jnp.exp(jnp.sum(lp[2:3] * lp[3:4], axis=1, keepdims=True)) + lam_init)
    ot = acc_s[0] / l_s[0] - lam * (acc_s[1] / l_s[1])
    o = ot.T
    y = _rms(o, sg_ref[...]) * (1.0 - lam_init) * _silu(bg_ref[...].astype(F32))
    y_ref[...] = y.astype(y_ref.dtype)


def _flash_call(qt, kk, vt, proj, lam_p, sg, *, bsz, s, lam_init, online_max):
    nq = s // BQ
    return pl.pallas_call(
        functools.partial(_flash_body, lam_init=lam_init, online_max=online_max),
        grid=(bsz, HEADS, nq),
        in_specs=[pl.BlockSpec((1, HD, BQ), lambda b, h, q: (b, h, q)),
                  pl.BlockSpec((s, HD), lambda b, h, q: (b, h)),
                  pl.BlockSpec((1, s // BK, HD, BK), lambda b, h, q: (b, 0, h, 0)),
                  pl.BlockSpec((BQ, HD), lambda b, h, q: (b * nq + q, C_BG * HEADS + h)),
                  pl.BlockSpec((4, B_DQK), lambda b, h, q: (0, 0)),
                  pl.BlockSpec((1, HD), lambda b, h, q: (0, 0))],
        out_specs=pl.BlockSpec((BQ, HD), lambda b, h, q: (b * nq + q, h)),
        out_shape=jax.ShapeDtypeStruct((bsz * s, GROUP), BF16),
        scratch_shapes=[pltpu.VMEM((2, HD, BQ), BF16), pltpu.VMEM((2, 1, BQ), F32),
                        pltpu.VMEM((2, 1, BQ), F32), pltpu.VMEM((2, HD, BQ), F32)],
        compiler_params=_cp(("arbitrary", "arbitrary", "arbitrary"), 48),
        name="diff_flash_online" if online_max else "diff_flash",
    )(qt, kk, vt, proj, lam_p, sg)


def _flash(qt, kk, vt, proj, lam_p, sg, qk_g, bsz, s, lam_init):
    bound = (B_DQK ** 0.5) * LOG2E * jnp.max(jnp.abs(qk_g[0])) * jnp.max(jnp.abs(qk_g[1]))
    args = (qt, kk, vt, proj, lam_p, sg)
    call = functools.partial(_flash_call, bsz=bsz, s=s, lam_init=lam_init)
    return lax.cond(bound <= MAX_UNSHIFTED_LOG2_SCORE - math.log2(s),
                    functools.partial(call, online_max=False),
                    functools.partial(call, online_max=True), *args)


def _memkv_body(mem_ref, g_ref, w_ref, kg_ref, km_ref, vm_ref):
    mn = _rms(mem_ref[0], g_ref[...]).astype(BF16)
    kv = _bdot(mn, w_ref[...])
    for h in range(HEADS):
        sl = slice(h * HD, (h + 1) * HD)
        km_ref[0, :, sl] = _rms(kv[:, sl], kg_ref[...]).astype(BF16)
    vm_ref[0] = kv[:, GROUP:].astype(BF16)


def _memkv(mem, g, w, kg):
    bsz = mem.shape[0]
    return pl.pallas_call(
        _memkv_body,
        grid=(bsz,),
        in_specs=[pl.BlockSpec((1, N_MEM, D_MODEL), lambda b: (b, 0, 0)),
                  pl.BlockSpec((1, D_MODEL), lambda b: (0, 0)),
                  pl.BlockSpec((D_MODEL, 2 * GROUP), lambda b: (0, 0)),
                  pl.BlockSpec((1, HD), lambda b: (0, 0))],
        out_specs=[pl.BlockSpec((1, N_MEM, GROUP), lambda b: (b, 0, 0))] * 2,
        out_shape=[jax.ShapeDtypeStruct((bsz, N_MEM, GROUP), BF16)] * 2,
        compiler_params=_cp(("arbitrary",), 32),
        name="mem_kv",
    )(mem, g, w, kg)


def _merge_body(x_ref, ya_ref, yb_ref, yc_ref, xq_ref, xg_ref, km_ref, vm_ref, qg_ref, w_ref, o_ref):
    acc = x_ref[...]
    acc = acc + _bdot(ya_ref[...], w_ref[0:GROUP, :])
    acc = acc + _bdot(yb_ref[...], w_ref[GROUP:2 * GROUP, :])
    acc = acc + _bdot(yc_ref[...], w_ref[2 * GROUP:3 * GROUP, :])
    qg = qg_ref[...]
    yx = []
    for h in range(HEADS):
        sl = slice(h * HD, (h + 1) * HD)
        qn = (_rms(xq_ref[:, sl].astype(F32), qg) * (HD ** -0.5)).astype(BF16)
        s = lax.dot_general(qn, km_ref[0, :, sl], NT, preferred_element_type=F32)
        p = jnp.exp(s - jnp.max(s, axis=-1, keepdims=True))
        l = jnp.sum(p, axis=-1, keepdims=True)
        ox = _bdot(p.astype(BF16), vm_ref[0, :, sl]) / l
        yx.append((ox * _silu(xg_ref[:, sl].astype(F32))).astype(BF16))
    o_ref[...] = acc + _bdot(jnp.concatenate(yx, axis=1), w_ref[3 * GROUP:4 * GROUP, :])


def _merge(x2, ya, yb, yc, proj, km, vm, qg, w_out, bsz, s):
    nb = s // M_BM
    t = bsz * s
    blk = lambda: pl.BlockSpec((M_BM, GROUP), lambda i: (i, 0))
    col = lambda cid: pl.BlockSpec((M_BM, GROUP), lambda i, cid=cid: (i, cid))
    return pl.pallas_call(
        _merge_body,
        grid=(t // M_BM,),
        in_specs=[pl.BlockSpec((M_BM, D_MODEL), lambda i: (i, 0)),
                  blk(), blk(), blk(), col(C_XQ), col(C_XG),
                  pl.BlockSpec((1, N_MEM, GROUP), lambda i: (i // nb, 0, 0)),
                  pl.BlockSpec((1, N_MEM, GROUP), lambda i: (i // nb, 0, 0)),
                  pl.BlockSpec((1, HD), lambda i: (0, 0)),
                  pl.BlockSpec((4 * GROUP, D_MODEL), lambda i: (0, 0))],
        out_specs=pl.BlockSpec((M_BM, D_MODEL), lambda i: (i, 0)),
        out_shape=jax.ShapeDtypeStruct((t, D_MODEL), F32),
        compiler_params=_cp(("arbitrary",), 48),
        name="xattn_out_proj",
    )(x2, ya, yb, yc, proj, proj, km, vm, qg, w_out)


def _layer(x2, mem, cosm, sinm, layer_idx, lb, norm_g, w_in, mlstm_gate_b, hgrn_norm_g,
           diff_qk_norm_g, diff_lambda, diff_subln_g, mlstm_conv_w, mlstm_conv_b,
           mlstm_norm_g, mem_norm_g, w_mem_kv, xattn_qk_norm_g, w_out, bsz, s):
    g12 = 12 * GROUP
    w_main = jnp.concatenate([w_in[:, :g12], w_in[:, g12 + 2 * HEADS:]], axis=1).astype(BF16)
    wgt = jnp.pad(w_in[:, g12:g12 + 2 * HEADS].T, ((0, 8), (0, 0))).astype(BF16)
    gq = jnp.tile(diff_qk_norm_g[0], GROUP // B_DQK)[None, :]
    gk = jnp.tile(diff_qk_norm_g[1], GROUP // B_DQK)[None, :]
    proj, grow, qt, kk, vt, ckt, avt = _inproj(x2, norm_g[None, :], w_main, wgt, cosm, sinm, gq, gk,
                                               mlstm_conv_w, mlstm_conv_b[None, :], bsz, s)

    ya, yc = _recurrent(proj, ckt, avt, grow, lb[None, :], hgrn_norm_g[None, :],
                        mlstm_gate_b[:, None], mlstm_norm_g[None, :], bsz, s)

    lam_init = 0.8 - 0.6 * math.exp(-0.3 * layer_idx)
    yb = _flash(qt, kk, vt, proj, diff_lambda, diff_subln_g[None, :], diff_qk_norm_g, bsz, s, lam_init)

    km, vm = _memkv(mem, mem_norm_g[None, :], w_mem_kv.astype(BF16), xattn_qk_norm_g[1][None, :])
    return _merge(x2, ya, yb, yc, proj, km, vm, xattn_qk_norm_g[0][None, :],
                  w_out.astype(BF16), bsz, s)


def kernel(x, mem, positions, norm_g, w_in, mlstm_gate_b, hgrn_lb_logits, hgrn_norm_g,
           diff_qk_norm_g, diff_lambda, diff_subln_g, mlstm_conv_w, mlstm_conv_b,
           mlstm_norm_g, mem_norm_g, w_mem_kv, xattn_qk_norm_g, w_out):
    bsz, s, d = x.shape
    depth = norm_g.shape[0]
    assert d == D_MODEL and mem.shape[1] == N_MEM
    assert all(s % blk == 0 for blk in (IN_BM, R_BLK, P_BM, BQ, BK, M_BM)) and (bsz * s) % T_BM == 0
    cosm, sinm = _rope_tables(positions)
    sm = jax.nn.softmax(hgrn_lb_logits.astype(F32), axis=0)
    lower_bounds = jnp.cumsum(sm, axis=0) - sm[0]
    x2 = x.reshape(bsz * s, d)
    for l in range(depth):
        x2 = _layer(x2, mem, cosm, sinm, l, lower_bounds[l], norm_g[l], w_in[l], mlstm_gate_b[l],
                    hgrn_norm_g[l], diff_qk_norm_g[l], diff_lambda[l], diff_subln_g[l],
                    mlstm_conv_w[l], mlstm_conv_b[l], mlstm_norm_g[l], mem_norm_g[l],
                    w_mem_kv[l], xattn_qk_norm_g[l], w_out[l], bsz, s)
    return x2.reshape(bsz, s, d)
```

```python
import functools
import math

import jax
import jax.numpy as jnp
from jax import lax
from jax.experimental import pallas as pl
from jax.experimental.pallas import tpu as pltpu

F32 = jnp.float32
BF16 = jnp.bfloat16
HI = lax.Precision.HIGHEST

D_MODEL = 1024
N_MEM = 256
GROUP = 512
HEADS = 4
HD = 128
B_DQK = 64
ROPE_DIM = 16
ROPE_THETA = 500000.0
CONV_K = 4
EPS = 1e-6
NEG = -1e30
TINY = 1e-30
LOG2E = math.log2(math.e)
MAX_UNSHIFTED_LOG2_SCORE = 100.0

LANES = 128
N_MAIN = 15 * GROUP

(W_AQ, W_AF, W_AI, W_AG, W_BQ, W_BK, W_BV, W_BG,
 W_CQ, W_CK, W_CV, W_CO, W_CG, W_XQ, W_XG) = range(15)
(C_AQ, C_AF, C_AI, C_AG, C_BG, C_CQ, C_CV, C_CO, C_CG, C_XQ, C_XG) = range(11)
OUT_COL = {W_AQ: C_AQ, W_AF: C_AF, W_AI: C_AI, W_AG: C_AG, W_BG: C_BG, W_CQ: C_CQ,
           W_CV: C_CV, W_CO: C_CO, W_CG: C_CG, W_XQ: C_XQ, W_XG: C_XG}
N_OUT = len(OUT_COL) * GROUP

IN_BM, IN_BN = 512, 1536
R_BLK = 512
A_CH, A_SUB = 128, 16
A_MAX_SUB_LOG2_DECAY = 60.0
C_CH = 128
P_BM = 512
BQ = 2048
BK = 1024
BKD = 512
M_BM = 512
T_BM = 1024

NT = (((1,), (1,)), ((), ()))
TN = (((0,), (0,)), ((), ()))


def _cp(sem, vmem_mib):
    return pltpu.CompilerParams(dimension_semantics=sem, vmem_limit_bytes=vmem_mib * 2 ** 20)


def _silu(x):
    return x * jax.nn.sigmoid(x)


def _log_sigmoid(x):
    return jnp.minimum(x, 0.0) - jnp.log1p(jnp.exp(-jnp.abs(x)))


def _rms(x, g):
    return x * lax.rsqrt(jnp.mean(x * x, axis=-1, keepdims=True) + EPS) * g


def _bdot(a, b):
    return jnp.dot(a, b, preferred_element_type=F32)


def _unit_rms_heads(xs, mean_ref):
    n = xs[0].shape[0]
    x = jnp.concatenate(xs, axis=0)
    y = x * lax.rsqrt(_bdot((x * x).astype(BF16), mean_ref[...]) + EPS)
    return [y[i * n:(i + 1) * n] for i in range(len(xs))]


def _cumsum_rows(x):
    row = lax.broadcasted_iota(jnp.int32, x.shape, 0)
    d = 1
    while d < x.shape[0]:
        x = x + jnp.where(row >= d, pltpu.roll(x, d, 0), 0.0)
        d *= 2
    return x


def _colmat(r):
    return jnp.broadcast_to(r, (r.shape[1], r.shape[1])).T


def _rope_body(pos_ref, invf_ref, cos_ref, sin_ref):
    ang = pos_ref[...].astype(F32) * invf_ref[...]
    lane = lax.broadcasted_iota(jnp.int32, ang.shape, 1) % B_DQK
    c, s = jnp.cos(ang), jnp.sin(ang)
    cos_ref[...] = jnp.where(lane < ROPE_DIM, c, 1.0)
    sin_ref[...] = jnp.where(lane < ROPE_DIM // 2, -s, jnp.where(lane < ROPE_DIM, s, 0.0))


def _rope_tables(positions):
    t = positions.size
    inv_freq = ROPE_THETA ** (-jnp.arange(0, ROPE_DIM, 2, dtype=F32) / ROPE_DIM)
    invf = jnp.tile(inv_freq, LANES // (ROPE_DIM // 2))[None, :]
    return pl.pallas_call(
        _rope_body,
        grid=(t // T_BM,),
        in_specs=[pl.BlockSpec((T_BM, 1), lambda i: (i, 0)),
                  pl.BlockSpec((1, LANES), lambda i: (0, 0))],
        out_specs=[pl.BlockSpec((T_BM, LANES), lambda i: (i, 0))] * 2,
        out_shape=[jax.ShapeDtypeStruct((t, LANES), F32)] * 2,
        compiler_params=_cp(("arbitrary",), 32),
        name="rope_tables",
    )(positions.reshape(t, 1), invf)


def _inproj_body(x_ref, g_ref, w_ref, wgt_ref, cos_ref, sin_ref, gq_ref, gk_ref, bd_ref, cw_ref, cb_ref,
                 o_ref, gr_ref, qt_ref, k_ref, vt_ref, ckt_ref, avt_ref, tmp_ref, carry_ref,
                 *, steps_per_seq):
    @pl.when(pl.program_id(0) == 0)
    def _():
        carry_ref[...] = jnp.zeros_like(carry_ref)

    h = _rms(x_ref[...], g_ref[...]).astype(BF16)
    gr = lax.dot_general(wgt_ref[...], h, NT, preferred_element_type=F32)
    for j in range(IN_BM // LANES):
        gr_ref[j] = gr[:8, j * LANES:(j + 1) * LANES]

    reps = GROUP // LANES
    cosm = jnp.concatenate([cos_ref[...]] * reps, axis=1)
    sinm = jnp.concatenate([sin_ref[...]] * reps, axis=1)
    lane = lax.broadcasted_iota(jnp.int32, (IN_BM, GROUP), 1)
    first_half = (lane % ROPE_DIM) < ROPE_DIM // 2

    def norm_rope(x, gain):
        ms = _bdot((x * x).astype(BF16), bd_ref[...])
        xn = x * lax.rsqrt(ms + EPS) * gain
        partner = jnp.where(first_half,
                            pltpu.roll(xn, GROUP - ROPE_DIM // 2, 1),
                            pltpu.roll(xn, ROPE_DIM // 2, 1))
        return xn * cosm + partner * sinm

    seq_start = pl.program_id(0) % steps_per_seq == 0

    def conv_silu(y, g):
        prev = jnp.where(seq_start, 0.0, carry_ref[g])
        carry_ref[g] = y[IN_BM - 8:, :]
        x = jnp.concatenate([prev, y], axis=0)
        w = cw_ref[:, g * GROUP:(g + 1) * GROUP]
        t = w[0:1] * x
        for j in range(1, CONV_K):
            t = w[j:j + 1] * x + pltpu.roll(t, 1, 0)
        return _silu(t[8:] + cb_ref[:, g * GROUP:(g + 1) * GROUP])

    def store_chunks_transposed(dst_ref, val):
        tmp_ref[...] = val
        for c in range(IN_BM // C_CH):
            dst_ref[0, c] = tmp_ref[c * C_CH:(c + 1) * C_CH, :].T.astype(BF16)

    def project(wid):
        return _bdot(h, w_ref[:, wid * GROUP:(wid + 1) * GROUP])

    def finish(wid, y):
        if wid == W_BQ:
            tmp_ref[...] = norm_rope(y, gq_ref[...]) * (B_DQK ** -0.5 * LOG2E)
            qt_ref[0] = tmp_ref[...].T.astype(BF16)
        elif wid == W_BK:
            k_ref[...] = norm_rope(y, gk_ref[...]).astype(BF16)
        elif wid == W_BV:
            tmp_ref[...] = y
            vt_ref[0, 0] = tmp_ref[...].T.astype(BF16)
        elif wid == W_CK:
            store_chunks_transposed(ckt_ref, conv_silu(y, 1) * (HD ** -0.5))
        else:
            if wid == W_CQ:
                y = conv_silu(y, 0)
            elif wid == W_AI:
                store_chunks_transposed(avt_ref, y)
            cid = OUT_COL[wid]
            o_ref[:, cid * GROUP:(cid + 1) * GROUP] = y.astype(o_ref.dtype)

    busy = (W_BQ, W_BK, W_CQ, W_CK, W_BV, W_AI)
    plain = [w for w in range(N_MAIN // GROUP) if w not in busy]
    ys = [project(wid) for wid in busy]
    for i, wid in enumerate(busy):
        finish(wid, ys[i])
        for p in plain[i * len(plain) // len(busy):(i + 1) * len(plain) // len(busy)]:
            finish(p, project(p))


def _inproj(x2, g, w_main, wgt, cosm, sinm, gq, gk, cw, cb, bsz, s):
    assert IN_BM == P_BM and IN_BM % C_CH == 0
    t = x2.shape[0]
    nb = s // P_BM
    ppk = BK // P_BM
    cpb = IN_BM // C_CH
    seg = jnp.arange(GROUP) // B_DQK
    bd = (jnp.where(seg[:, None] == seg[None, :], 1.0 / B_DQK, 0.0)).astype(BF16)
    chunk_t = pl.BlockSpec((1, cpb, GROUP, C_CH), lambda i: (i // nb, i % nb, 0, 0))
    return pl.pallas_call(
        functools.partial(_inproj_body, steps_per_seq=nb),
        grid=(t // IN_BM,),
        in_specs=[pl.BlockSpec((IN_BM, D_MODEL), lambda i: (i, 0)),
                  pl.BlockSpec((1, D_MODEL), lambda i: (0, 0)),
                  pl.BlockSpec((D_MODEL, N_MAIN), lambda i: (0, 0), pipeline_mode=pl.Buffered(1)),
                  pl.BlockSpec((16, D_MODEL), lambda i: (0, 0)),
                  pl.BlockSpec((IN_BM, LANES), lambda i: (i, 0)),
                  pl.BlockSpec((IN_BM, LANES), lambda i: (i, 0)),
                  pl.BlockSpec((1, GROUP), lambda i: (0, 0)),
                  pl.BlockSpec((1, GROUP), lambda i: (0, 0)),
                  pl.BlockSpec((GROUP, GROUP), lambda i: (0, 0)),
                  pl.BlockSpec((CONV_K, 2 * GROUP), lambda i: (0, 0)),
                  pl.BlockSpec((1, 2 * GROUP), lambda i: (0, 0))],
        out_specs=[pl.BlockSpec((IN_BM, N_OUT), lambda i: (i, 0)),
                   pl.BlockSpec((IN_BM // LANES, 8, LANES), lambda i: (i, 0, 0)),
                   pl.BlockSpec((1, GROUP, P_BM), lambda i: (i // nb, 0, i % nb)),
                   pl.BlockSpec((P_BM, GROUP), lambda i: (i, 0)),
                   pl.BlockSpec((1, 1, GROUP, P_BM),
                                lambda i: (i // nb, (i % nb) // ppk, 0, (i % nb) % ppk)),
                   chunk_t, chunk_t],
        out_shape=[jax.ShapeDtypeStruct((t, N_OUT), BF16),
                   jax.ShapeDtypeStruct((t // LANES, 8, LANES), F32),
                   jax.ShapeDtypeStruct((bsz, GROUP, s), BF16),
                   jax.ShapeDtypeStruct((t, GROUP), BF16),
                   jax.ShapeDtypeStruct((bsz, s // BK, GROUP, BK), BF16),
                   jax.ShapeDtypeStruct((bsz, s // C_CH, GROUP, C_CH), BF16),
                   jax.ShapeDtypeStruct((bsz, s // C_CH, GROUP, C_CH), BF16)],
        scratch_shapes=[pltpu.VMEM((IN_BM, GROUP), F32),
                        pltpu.VMEM((2, 8, GROUP), F32)],
        compiler_params=_cp(("arbitrary",), 56),
        name="in_proj",
    )(x2, g, w_main, wgt, cosm, sinm, gq, gk, bd, cw, cb)


def _hgrn_gates(rows, af_ref, lb, oml, lf_s, ka_s):
    sg = jax.nn.sigmoid(af_ref[rows, :].astype(F32))
    logf = jnp.log2(jnp.maximum(lb + oml * sg, TINY))
    lf_s[...] = logf
    ka_s[...] = oml * (1.0 - sg)
    sums = jnp.sum(logf.reshape(A_CH // A_SUB, A_SUB, GROUP), axis=1)
    return -jnp.min(sums)


def _hgrn_prep(rows, aq_ref, ai_ref, ag_ref, lf_s, ka_s):
    q = _silu(aq_ref[rows, :].astype(F32)) * (HD ** -0.5)
    v = ai_ref[rows, :].astype(F32)
    gate = _silu(ag_ref[rows, :].astype(F32))
    return q, ka_s[...], v, gate, _cumsum_rows(lf_s[...])


def _hgrn_finish(pre, st_ref, vt_ref, bounded):
    q, ka, v, gate, b = pre
    nsub = A_CH // A_SUB
    half = A_SUB // 2
    row = lax.broadcasted_iota(jnp.int32, (A_CH, HD), 0)
    tsub = lax.broadcasted_iota(jnp.int32, (nsub, half, HD), 1)
    ti = lax.broadcasted_iota(jnp.int32, (A_CH, A_CH), 0)
    si = lax.broadcasted_iota(jnp.int32, (A_CH, A_CH), 1)
    blast = b[A_CH - 1:A_CH, :]
    qdec = q * jnp.exp2(b)
    kdec = ka * jnp.exp2(blast - b)
    sdec = jnp.exp2(blast)
    outs = []
    for h in range(HEADS):
        sl = slice(h * HD, (h + 1) * HD)
        qh, kh, vh, bh = q[:, sl], ka[:, sl], v[:, sl], b[:, sl]
        vhb = vh.astype(BF16)
        st = st_ref[h]
        o = lax.dot_general(qdec[:, sl].astype(BF16), st.astype(BF16), NT, preferred_element_type=F32)
        parts = []
        for i in range(nsub):
            lo = i * A_SUB
            hi = lo + A_SUB if bounded else lo
            if hi == 0:
                parts.append(jnp.zeros((A_SUB, A_CH), F32))
                continue
            bi = bh[lo - 1:lo, :] if i else jnp.zeros((1, HD), F32)
            qi = (qh[lo:lo + A_SUB] * jnp.exp2(bh[lo:lo + A_SUB] - bi)).astype(BF16)
            kp = (kh[:hi] * jnp.exp2(bi - bh[:hi])).astype(BF16)
            part = lax.dot_general(qi, kp, NT, preferred_element_type=F32)
            if hi < A_CH:
                part = jnp.concatenate([part, jnp.zeros((A_SUB, A_CH - hi), F32)], axis=1)
            parts.append(part)
        amat = jnp.concatenate(parts, axis=0)
        if bounded:
            amat = jnp.where(si <= ti, amat, 0.0)
        o = o + _bdot(amat.astype(BF16), vhb)
        if not bounded:
            q3 = qh.reshape(nsub, A_SUB, HD)
            k3 = kh.reshape(nsub, A_SUB, HD)
            b3 = bh.reshape(nsub, A_SUB, HD)
            v3 = vh.reshape(nsub, A_SUB, HD)
            q_lo, q_hi = q3[:, :half], q3[:, half:]
            b_lo, b_hi = b3[:, :half], b3[:, half:]
            acc_lo = jnp.zeros((nsub, half, HD), F32)
            acc_hi = jnp.zeros((nsub, half, HD), F32)
            for s in range(A_SUB):
                bs, ks, vs = b3[:, s:s + 1, :], k3[:, s:s + 1, :], v3[:, s:s + 1, :]
                if s < half:
                    e_lo = jnp.exp2(jnp.where(tsub >= s, b_lo - bs, NEG))
                    acc_lo = acc_lo + jnp.sum(q_lo * ks * e_lo, axis=-1, keepdims=True) * vs
                    e_hi = jnp.exp2(b_hi - bs)
                else:
                    e_hi = jnp.exp2(jnp.where(tsub >= s - half, b_hi - bs, NEG))
                acc_hi = acc_hi + jnp.sum(q_hi * ks * e_hi, axis=-1, keepdims=True) * vs
            o = o + jnp.concatenate([acc_lo, acc_hi], axis=1).reshape(A_CH, HD)
        st_ref[h] = st * sdec[:, sl] + _bdot(vt_ref[sl, :], kdec[:, sl].astype(BF16))
        outs.append(o)
    return outs


def _mlstm_gate_rows(grow_ref, gbc, triu_ref):
    nc = grow_ref.shape[0]
    g = grow_ref[...] + gbc
    srow = lax.broadcasted_iota(jnp.int32, g.shape, 1)
    g = jnp.where(srow < HEADS, g, _log_sigmoid(g)) * LOG2E
    cum = jnp.dot(g.reshape(nc * 8, LANES), triu_ref[...], precision=HI,
                  preferred_element_type=F32).reshape(nc, 8, LANES)
    b = cum[:, HEADS:, :]
    ab = jnp.concatenate([g[:, :HEADS, :] - b, b], axis=1)
    cm = ab.reshape(nc * 8, LANES)
    lane = lax.broadcasted_iota(jnp.int32, cm.shape, 1)
    d = 1
    while d < LANES:
        cm = jnp.maximum(cm, jnp.where(lane >= d, pltpu.roll(cm, d, 1), NEG))
        d *= 2
    return ab, cm.reshape(nc, 8, LANES)


def _mlstm_vectors(ab_ref, cm_ref, m_s):
    ab = ab_ref[...]
    a4, b4 = ab[:HEADS], ab[HEADS:]
    m_prev = m_s[0:HEADS, :]
    mx4 = jnp.maximum(cm_ref[0:HEADS, :], m_prev)
    mt4 = b4 + mx4
    mx_last = jnp.broadcast_to(mx4[:, C_CH - 1:C_CH], (HEADS, LANES))
    wg4 = jnp.exp2(a4 - mx_last)
    dec4 = jnp.exp2(m_prev - mx_last)
    m_s[0:HEADS, :] = jnp.broadcast_to(mt4[:, C_CH - 1:C_CH], (HEADS, LANES))
    cols = [(_colmat(mx4[h:h + 1]), _colmat(mt4[h:h + 1])) for h in range(HEADS)]
    return a4, wg4, dec4, m_prev, cols


def _mlstm_chunk(rows, per_batch, vecs):
    ri = lax.broadcasted_iota(jnp.int32, (C_CH, C_CH), 0)
    ci = lax.broadcasted_iota(jnp.int32, (C_CH, C_CH), 1)
    causal = ri >= ci
    ones_blk = jnp.ones((C_CH, HD), BF16)
    hcs = []
    for (cv_ref, co_ref, cq_ref, kt_ref, ct_s), (a4, wg4, dec4, m_prev, cols) in zip(per_batch, vecs):
        for h in range(HEADS):
            sl = slice(h * HD, (h + 1) * HD)
            row = slice(h, h + 1)
            mxm, mtm = cols[h]
            ct = ct_s[h]
            qb = cq_ref[rows, sl]
            qf = qb.astype(F32)
            ktb = kt_ref[sl, :]
            kt = ktb.astype(F32)
            vaug = jnp.concatenate([cv_ref[rows, sl], ones_blk], axis=1)

            w = jnp.exp2(jnp.where(causal, a4[row] - mxm, NEG))
            sc = _bdot(qb, ktb)
            lhs = jnp.concatenate([(sc * w).astype(BF16),
                                   (qf * jnp.exp2(m_prev[row] - mxm)).astype(BF16)], axis=1)
            tot = _bdot(lhs, jnp.concatenate([vaug, ct.astype(BF16)], axis=0))
            hout = tot[:, :HD] / jnp.maximum(jnp.abs(tot[:, HD:]), jnp.exp2(-mtm))

            dec2 = jnp.concatenate([dec4[row], dec4[row]], axis=1)
            ct_s[h] = dec2 * ct + _bdot((kt * wg4[row]).astype(BF16), vaug)
            hcs.append(jax.nn.sigmoid(co_ref[rows, sl].astype(F32)) * hout)
    return hcs


def _rec_body(aq_ref, af_ref, ai_ref, ag_ref, cq_ref, cv_ref, co_ref, cg_ref, kt_ref, vt_ref, grow_ref,
              lb_ref, ang_ref, gbc_ref, cng_ref, triu_ref, mean_ref,
              ya_ref, yc_ref, st_s, ab_s, cm_s, ct_s, m_s, lf_s, ka_s):
    bsz = aq_ref.shape[0]
    nchunk = R_BLK // C_CH

    @pl.when(pl.program_id(0) == 0)
    def _():
        st_s[...] = jnp.zeros_like(st_s)
        ct_s[...] = jnp.zeros_like(ct_s)
        m_s[...] = jnp.zeros_like(m_s)

    gbc = gbc_ref[...]
    for bi in range(bsz):
        ab_s[bi], cm_s[bi] = _mlstm_gate_rows(grow_ref.at[bi], gbc, triu_ref)

    lb = lb_ref[...]
    oml = 1.0 - lb
    ang = ang_ref[...]
    cng = cng_ref[...]

    def chunk(c, carry):
        rows = pl.ds(pl.multiple_of(c * C_CH, C_CH), C_CH)
        spans = [_hgrn_gates(rows, af_ref.at[bi], lb, oml, lf_s.at[bi], ka_s.at[bi]) for bi in range(bsz)]
        bounded = functools.reduce(jnp.maximum, spans) <= A_MAX_SUB_LOG2_DECAY

        def rest(is_bounded):
            vecs = [_mlstm_vectors(ab_s.at[bi, c], cm_s.at[bi, c], m_s.at[bi]) for bi in range(bsz)]
            pres = [_hgrn_prep(rows, aq_ref.at[bi], ai_ref.at[bi], ag_ref.at[bi], lf_s.at[bi], ka_s.at[bi])
                    for bi in range(bsz)]
            outs = [o for bi in range(bsz)
                    for o in _hgrn_finish(pres[bi], st_s.at[bi], vt_ref.at[bi, c], is_bounded)]
            outs = _unit_rms_heads(outs, mean_ref)
            per_batch = [(cv_ref.at[bi], co_ref.at[bi], cq_ref.at[bi], kt_ref.at[bi, c], ct_s.at[bi])
                         for bi in range(bsz)]
            hcs = _unit_rms_heads(_mlstm_chunk(rows, per_batch, vecs), mean_ref)
            for i in range(bsz * HEADS):
                bi, h = divmod(i, HEADS)
                sl = slice(h * HD, (h + 1) * HD)
                ya_ref[bi, rows, sl] = (outs[i] * ang * pres[bi][3][:, sl]).astype(ya_ref.dtype)
                yc_ref[bi, rows, sl] = (hcs[i] * cng * _silu(cg_ref[bi, rows, sl].astype(F32))
                                        ).astype(yc_ref.dtype)

        pl.when(bounded)(functools.partial(rest, True))
        pl.when(jnp.logical_not(bounded))(functools.partial(rest, False))
        return carry

    lax.fori_loop(0, nchunk, chunk, 0)


def _recurrent(proj, ckt, avt, grow, lb, ang, gbc, cng, bsz, s):
    assert A_CH == C_CH
    proj3 = proj.reshape(bsz, s, N_OUT)
    nchunk = R_BLK // C_CH
    col = lambda cid: pl.BlockSpec((bsz, R_BLK, GROUP), lambda t, cid=cid: (0, t, cid))
    full = lambda shp: pl.BlockSpec(shp, lambda t: (0,) * len(shp))
    chunk_t = pl.BlockSpec((bsz, nchunk, GROUP, C_CH), lambda t: (0, t, 0, 0))
    triu = jnp.triu(jnp.ones((C_CH, C_CH), F32))
    mean_mat = jnp.full((HD, HD), 1.0 / HD, BF16)
    out = pl.BlockSpec((bsz, R_BLK, GROUP), lambda t: (0, t, 0))
    ya, yc = pl.pallas_call(
        _rec_body,
        grid=(s // R_BLK,),
        in_specs=[col(C_AQ), col(C_AF), col(C_AI), col(C_AG),
                  col(C_CQ), col(C_CV), col(C_CO), col(C_CG), chunk_t, chunk_t,
                  pl.BlockSpec((bsz, nchunk, 8, LANES), lambda t: (0, t, 0, 0)),
                  full((1, GROUP)), full((1, HD)),
                  full((8, 1)), full((1, HD)),
                  full((C_CH, C_CH)), full((HD, HD))],
        out_specs=[out, out],
        out_shape=[jax.ShapeDtypeStruct((bsz, s, GROUP), BF16)] * 2,
        scratch_shapes=[pltpu.VMEM((bsz, HEADS, HD, HD), F32),
                        pltpu.VMEM((bsz, nchunk, 8, LANES), F32), pltpu.VMEM((bsz, nchunk, 8, LANES), F32),
                        pltpu.VMEM((bsz, HEADS, HD, 2 * HD), F32), pltpu.VMEM((bsz, 8, LANES), F32),
                        pltpu.VMEM((bsz, A_CH, GROUP), F32), pltpu.VMEM((bsz, A_CH, GROUP), F32)],
        compiler_params=_cp(("arbitrary",), 48),
        name="hgrn2_mlstm",
    )(*([proj3] * 8), ckt, avt, grow.reshape(bsz, s // LANES, 8, LANES),
      lb, ang, gbc, cng, triu, mean_mat)
    return ya.reshape(bsz * s, GROUP), yc.reshape(bsz * s, GROUP)


def _flash_body(qt_ref, k_ref, vt_ref, bg_ref, lam_ref, sg_ref, y_ref,
                qz_s, m_s, l_s, acc_s, *, lam_init, online_max):
    qi = pl.program_id(2)
    qt = qt_ref[0]
    rowi = lax.broadcasted_iota(jnp.int32, (HD, BQ), 0)
    zero = jnp.zeros_like(qt)
    qz_s[0] = jnp.where(rowi < B_DQK, qt, zero)
    qz_s[1] = jnp.where(rowi >= B_DQK, qt, zero)
    m_s[...] = jnp.full_like(m_s, NEG)
    l_s[...] = jnp.zeros_like(l_s)
    acc_s[...] = jnp.zeros_like(acc_s)

    def step(kblk, vblk, q0, mask):
        qs = slice(q0, BQ)
        scores = [_bdot(kblk, qz_s[c, :, qs]) for c in range(2)]
        for c in range(2):
            s = scores[c]
            if mask is not None:
                s = jnp.where(mask, s, NEG)
            if online_max:
                m_old = m_s[c, :, qs]
                m_new = jnp.maximum(m_old, jnp.max(s, axis=0, keepdims=True))
                alpha = jnp.exp2(m_old - m_new)
                p = jnp.exp2(s - m_new)
                l_s[c, :, qs] = alpha * l_s[c, :, qs] + jnp.sum(p, axis=0, keepdims=True)
                acc_s[c, :, qs] = alpha * acc_s[c, :, qs] + _bdot(vblk, p.astype(BF16))
                m_s[c, :, qs] = m_new
            else:
                p = jnp.exp2(s)
                l_s[c, :, qs] = l_s[c, :, qs] + jnp.sum(p, axis=0, keepdims=True)
                acc_s[c, :, qs] = acc_s[c, :, qs] + _bdot(vblk, p.astype(BF16))

    def body(kb, carry):
        r0 = pl.multiple_of(kb * BK, BK)
        step(k_ref[pl.ds(r0, BK), :], vt_ref[0, kb], 0, None)
        return carry

    kpq = BQ // BK
    lax.fori_loop(0, qi * kpq, body, 0)
    for d in range(BQ // BKD):
        q0 = d * BKD
        kidx = lax.broadcasted_iota(jnp.int32, (BKD, BQ - q0), 0)
        qidx = lax.broadcasted_iota(jnp.int32, (BKD, BQ - q0), 1)
        rows = pl.ds(pl.multiple_of(qi * BQ + q0, BKD), BKD)
        lo = q0 % BK
        vblk = vt_ref[0, qi * kpq + q0 // BK][:, lo:lo + BKD]
        step(k_ref[rows, :], vblk, q0, kidx <= qidx)

    lp = lam_ref[...]
    lam = (jnp.exp(jnp.sum(lp[0:1] * lp[1:2], axis=1, keepdims=True))
           - jnp.exp(jnp.sum(lp[2:3] * lp[3:4], axis=1, keepdims=True)) + lam_init)
    ot = acc_s[0] / l_s[0] - lam * (acc_s[1] / l_s[1])
    o = ot.T
    y = _rms(o, sg_ref[...]) * (1.0 - lam_init) * _silu(bg_ref[...].astype(F32))
    y_ref[...] = y.astype(y_ref.dtype)


def _flash_call(qt, kk, vt, proj, lam_p, sg, *, bsz, s, lam_init, online_max):
    nq = s // BQ
    return pl.pallas_call(
        functools.partial(_flash_body, lam_init=lam_init, online_max=online_max),
        grid=(bsz, HEADS, nq),
        in_specs=[pl.BlockSpec((1, HD, BQ), lambda b, h, q: (b, h, q)),
                  pl.BlockSpec((s, HD), lambda b, h, q: (b, h)),
                  pl.BlockSpec((1, s // BK, HD, BK), lambda b, h, q: (b, 0, h, 0)),
                  pl.BlockSpec((BQ, HD), lambda b, h, q: (b * nq + q, C_BG * HEADS + h)),
                  pl.BlockSpec((4, B_DQK), lambda b, h, q: (0, 0)),
                  pl.BlockSpec((1, HD), lambda b, h, q: (0, 0))],
        out_specs=pl.BlockSpec((BQ, HD), lambda b, h, q: (b * nq + q, h)),
        out_shape=jax.ShapeDtypeStruct((bsz * s, GROUP), BF16),
        scratch_shapes=[pltpu.VMEM((2, HD, BQ), BF16), pltpu.VMEM((2, 1, BQ), F32),
                        pltpu.VMEM((2, 1, BQ), F32), pltpu.VMEM((2, HD, BQ), F32)],
        compiler_params=_cp(("arbitrary", "arbitrary", "arbitrary"), 48),
        name="diff_flash_online" if online_max else "diff_flash",
    )(qt, kk, vt, proj, lam_p, sg)


def _flash(qt, kk, vt, proj, lam_p, sg, qk_g, bsz, s, lam_init):
    bound = (B_DQK ** 0.5) * LOG2E * jnp.max(jnp.abs(qk_g[0])) * jnp.max(jnp.abs(qk_g[1]))
    args = (qt, kk, vt, proj, lam_p, sg)
    call = functools.partial(_flash_call, bsz=bsz, s=s, lam_init=lam_init)
    return lax.cond(bound <= MAX_UNSHIFTED_LOG2_SCORE - math.log2(s),
                    functools.partial(call, online_max=False),
                    functools.partial(call, online_max=True), *args)


def _memkv_body(mem_ref, g_ref, w_ref, kg_ref, km_ref, vm_ref):
    mn = _rms(mem_ref[0], g_ref[...]).astype(BF16)
    kv = _bdot(mn, w_ref[...])
    for h in range(HEADS):
        sl = slice(h * HD, (h + 1) * HD)
        km_ref[0, :, sl] = _rms(kv[:, sl], kg_ref[...]).astype(BF16)
    vm_ref[0] = kv[:, GROUP:].astype(BF16)


def _memkv(mem, g, w, kg):
    bsz = mem.shape[0]
    return pl.pallas_call(
        _memkv_body,
        grid=(bsz,),
        in_specs=[pl.BlockSpec((1, N_MEM, D_MODEL), lambda b: (b, 0, 0)),
                  pl.BlockSpec((1, D_MODEL), lambda b: (0, 0)),
                  pl.BlockSpec((D_MODEL, 2 * GROUP), lambda b: (0, 0)),
                  pl.BlockSpec((1, HD), lambda b: (0, 0))],
        out_specs=[pl.BlockSpec((1, N_MEM, GROUP), lambda b: (b, 0, 0))] * 2,
        out_shape=[jax.ShapeDtypeStruct((bsz, N_MEM, GROUP), BF16)] * 2,
        compiler_params=_cp(("arbitrary",), 32),
        name="mem_kv",
    )(mem, g, w, kg)


def _merge_body(x_ref, ya_ref, yb_ref, yc_ref, xq_ref, xg_ref, km_ref, vm_ref, qg_ref, w_ref, o_ref):
    acc = x_ref[...]
    acc = acc + _bdot(ya_ref[...], w_ref[0:GROUP, :])
    acc = acc + _bdot(yb_ref[...], w_ref[GROUP:2 * GROUP, :])
    acc = acc + _bdot(yc_ref[...], w_ref[2 * GROUP:3 * GROUP, :])
    qg = qg_ref[...]
    yx = []
    for h in range(HEADS):
        sl = slice(h * HD, (h + 1) * HD)
        qn = (_rms(xq_ref[:, sl].astype(F32), qg) * (HD ** -0.5)).astype(BF16)
        s = lax.dot_general(qn, km_ref[0, :, sl], NT, preferred_element_type=F32)
        p = jnp.exp(s - jnp.max(s, axis=-1, keepdims=True))
        l = jnp.sum(p, axis=-1, keepdims=True)
        ox = _bdot(p.astype(BF16), vm_ref[0, :, sl]) / l
        yx.append((ox * _silu(xg_ref[:, sl].astype(F32))).astype(BF16))
    o_ref[...] = acc + _bdot(jnp.concatenate(yx, axis=1), w_ref[3 * GROUP:4 * GROUP, :])


def _merge(x2, ya, yb, yc, proj, km, vm, qg, w_out, bsz, s):
    nb = s // M_BM
    t = bsz * s
    blk = lambda: pl.BlockSpec((M_BM, GROUP), lambda i: (i, 0))
    col = lambda cid: pl.BlockSpec((M_BM, GROUP), lambda i, cid=cid: (i, cid))
    return pl.pallas_call(
        _merge_body,
        grid=(t // M_BM,),
        in_specs=[pl.BlockSpec((M_BM, D_MODEL), lambda i: (i, 0)),
                  blk(), blk(), blk(), col(C_XQ), col(C_XG),
                  pl.BlockSpec((1, N_MEM, GROUP), lambda i: (i // nb, 0, 0)),
                  pl.BlockSpec((1, N_MEM, GROUP), lambda i: (i // nb, 0, 0)),
                  pl.BlockSpec((1, HD), lambda i: (0, 0)),
                  pl.BlockSpec((4 * GROUP, D_MODEL), lambda i: (0, 0))],
        out_specs=pl.BlockSpec((M_BM, D_MODEL), lambda i: (i, 0)),
        out_shape=jax.ShapeDtypeStruct((t, D_MODEL), F32),
        compiler_params=_cp(("arbitrary",), 48),
        name="xattn_out_proj",
    )(x2, ya, yb, yc, proj, proj, km, vm, qg, w_out)


def _layer(x2, mem, cosm, sinm, layer_idx, lb, norm_g, w_in, mlstm_gate_b, hgrn_norm_g,
           diff_qk_norm_g, diff_lambda, diff_subln_g, mlstm_conv_w, mlstm_conv_b,
           mlstm_norm_g, mem_norm_g, w_mem_kv, xattn_qk_norm_g, w_out, bsz, s):
    g12 = 12 * GROUP
    w_main = jnp.concatenate([w_in[:, :g12], w_in[:, g12 + 2 * HEADS:]], axis=1).astype(BF16)
    wgt = jnp.pad(w_in[:, g12:g12 + 2 * HEADS].T, ((0, 8), (0, 0))).astype(BF16)
    gq = jnp.tile(diff_qk_norm_g[0], GROUP // B_DQK)[None, :]
    gk = jnp.tile(diff_qk_norm_g[1], GROUP // B_DQK)[None, :]
    proj, grow, qt, kk, vt, ckt, avt = _inproj(x2, norm_g[None, :], w_main, wgt, cosm, sinm, gq, gk,
                                               mlstm_conv_w, mlstm_conv_b[None, :], bsz, s)

    ya, yc = _recurrent(proj, ckt, avt, grow, lb[None, :], hgrn_norm_g[None, :],
                        mlstm_gate_b[:, None], mlstm_norm_g[None, :], bsz, s)

    lam_init = 0.8 - 0.6 * math.exp(-0.3 * layer_idx)
    yb = _flash(qt, kk, vt, proj, diff_lambda, diff_subln_g[None, :], diff_qk_norm_g, bsz, s, lam_init)

    km, vm = _memkv(mem, mem_norm_g[None, :], w_mem_kv.astype(BF16), xattn_qk_norm_g[1][None, :])
    return _merge(x2, ya, yb, yc, proj, km, vm, xattn_qk_norm_g[0][None, :],
                  w_out.astype(BF16), bsz, s)


def kernel(x, mem, positions, norm_g, w_in, mlstm_gate_b, hgrn_lb_logits, hgrn_norm_g,
           diff_qk_norm_g, diff_lambda, diff_subln_g, mlstm_conv_w, mlstm_conv_b,
           mlstm_norm_g, mem_norm_g, w_mem_kv, xattn_qk_norm_g, w_out):
    bsz, s, d = x.shape
    depth = norm_g.shape[0]
    assert d == D_MODEL and mem.shape[1] == N_MEM
    assert all(s % blk == 0 for blk in (IN_BM, R_BLK, P_BM, BQ, BK, M_BM)) and (bsz * s) % T_BM == 0
    cosm, sinm = _rope_tables(positions)
    sm = jax.nn.softmax(hgrn_lb_logits.astype(F32), axis=0)
    lower_bounds = jnp.cumsum(sm, axis=0) - sm[0]
    x2 = x.reshape(bsz * s, d)
    for l in range(depth):
        x2 = _layer(x2, mem, cosm, sinm, l, lower_bounds[l], norm_g[l], w_in[l], mlstm_gate_b[l],
                    hgrn_norm_g[l], diff_qk_norm_g[l], diff_lambda[l], diff_subln_g[l],
                    mlstm_conv_w[l], mlstm_conv_b[l], mlstm_norm_g[l], mem_norm_g[l],
                    w_mem_kv[l], xattn_qk_norm_g[l], w_out[l], bsz, s)
    return x2.reshape(bsz, s, d)
```

```python
import functools
import math

import jax
import jax.numpy as jnp
from jax import lax
from jax.experimental import pallas as pl
from jax.experimental.pallas import tpu as pltpu

F32 = jnp.float32
BF16 = jnp.bfloat16
HI = lax.Precision.HIGHEST

D_MODEL = 1024
N_MEM = 256
GROUP = 512
HEADS = 4
HD = 128
B_DQK = 64
ROPE_DIM = 16
ROPE_THETA = 500000.0
CONV_K = 4
EPS = 1e-6
NEG = -1e30
TINY = 1e-30
LOG2E = math.log2(math.e)
MAX_UNSHIFTED_LOG2_SCORE = 100.0

LANES = 128
N_MAIN = 15 * GROUP
N_BEFORE_GATES = 12

(W_AQ, W_AF, W_AI, W_AG, W_BQ, W_BK, W_BV, W_BG,
 W_CQ, W_CK, W_CV, W_CO, W_CG, W_XQ, W_XG) = range(15)
(C_AQ, C_AF, C_AI, C_AG, C_BG, C_CQ, C_CV, C_CO, C_CG, C_XQ, C_XG) = range(11)
OUT_COL = {W_AQ: C_AQ, W_AF: C_AF, W_AI: C_AI, W_AG: C_AG, W_BG: C_BG, W_CQ: C_CQ,
           W_CV: C_CV, W_CO: C_CO, W_CG: C_CG, W_XQ: C_XQ, W_XG: C_XG}
N_OUT = len(OUT_COL) * GROUP

IN_BM, IN_BN = 512, 1536
R_BLK = 512
A_CH, A_SUB = 128, 16
A_MAX_SUB_LOG2_DECAY = 60.0
C_CH = 128
P_BM = 512
BQ = 2048
BK = 1024
BKD = 512
M_BM = 512
T_BM = 1024

NT = (((1,), (1,)), ((), ()))
TN = (((0,), (0,)), ((), ()))


def _cp(sem, vmem_mib):
    return pltpu.CompilerParams(dimension_semantics=sem, vmem_limit_bytes=vmem_mib * 2 ** 20)


def _silu(x):
    return x * jax.nn.sigmoid(x)


def _log_sigmoid(x):
    return jnp.minimum(x, 0.0) - jnp.log1p(jnp.exp(-jnp.abs(x)))


def _rms(x, g):
    return x * lax.rsqrt(jnp.mean(x * x, axis=-1, keepdims=True) + EPS) * g


def _bdot(a, b):
    return jnp.dot(a, b, preferred_element_type=F32)


def _unit_rms_heads(xs, mean_ref):
    n = xs[0].shape[0]
    x = jnp.concatenate(xs, axis=0)
    y = x * lax.rsqrt(_bdot((x * x).astype(BF16), mean_ref[...]) + EPS)
    return [y[i * n:(i + 1) * n] for i in range(len(xs))]


def _cumsum_rows(x):
    row = lax.broadcasted_iota(jnp.int32, x.shape, 0)
    d = 1
    while d < x.shape[0]:
        x = x + jnp.where(row >= d, pltpu.roll(x, d, 0), 0.0)
        d *= 2
    return x


def _colmat(r):
    return jnp.broadcast_to(r, (r.shape[1], r.shape[1])).T


def _rope_body(pos_ref, invf_ref, cos_ref, sin_ref):
    ang = pos_ref[...].astype(F32) * invf_ref[...]
    lane = lax.broadcasted_iota(jnp.int32, ang.shape, 1) % B_DQK
    c, s = jnp.cos(ang), jnp.sin(ang)
    cos_ref[...] = jnp.where(lane < ROPE_DIM, c, 1.0)
    sin_ref[...] = jnp.where(lane < ROPE_DIM // 2, -s, jnp.where(lane < ROPE_DIM, s, 0.0))


def _rope_tables(positions):
    t = positions.size
    inv_freq = ROPE_THETA ** (-jnp.arange(0, ROPE_DIM, 2, dtype=F32) / ROPE_DIM)
    invf = jnp.tile(inv_freq, LANES // (ROPE_DIM // 2))[None, :]
    return pl.pallas_call(
        _rope_body,
        grid=(t // T_BM,),
        in_specs=[pl.BlockSpec((T_BM, 1), lambda i: (i, 0)),
                  pl.BlockSpec((1, LANES), lambda i: (0, 0))],
        out_specs=[pl.BlockSpec((T_BM, LANES), lambda i: (i, 0))] * 2,
        out_shape=[jax.ShapeDtypeStruct((t, LANES), F32)] * 2,
        compiler_params=_cp(("arbitrary",), 32),
        name="rope_tables",
    )(positions.reshape(t, 1), invf)


def _inproj_body(x_ref, g_ref, wa_ref, wb_ref, wgt_ref, cos_ref, sin_ref, gq_ref, gk_ref, bd_ref, cw_ref, cb_ref,
                 o_ref, gr_ref, qt_ref, k_ref, vt_ref, ckt_ref, avt_ref, tmp_ref, carry_ref,
                 *, steps_per_seq):
    @pl.when(pl.program_id(0) == 0)
    def _():
        carry_ref[...] = jnp.zeros_like(carry_ref)

    h = _rms(x_ref[...], g_ref[...]).astype(BF16)
    gr = lax.dot_general(wgt_ref[...], h, NT, preferred_element_type=F32)
    for j in range(IN_BM // LANES):
        gr_ref[j] = gr[:8, j * LANES:(j + 1) * LANES]

    reps = GROUP // LANES
    cosm = jnp.concatenate([cos_ref[...]] * reps, axis=1)
    sinm = jnp.concatenate([sin_ref[...]] * reps, axis=1)
    lane = lax.broadcasted_iota(jnp.int32, (IN_BM, GROUP), 1)
    first_half = (lane % ROPE_DIM) < ROPE_DIM // 2

    def norm_rope(x, gain):
        ms = _bdot((x * x).astype(BF16), bd_ref[...])
        xn = x * lax.rsqrt(ms + EPS) * gain
        partner = jnp.where(first_half,
                            pltpu.roll(xn, GROUP - ROPE_DIM // 2, 1),
                            pltpu.roll(xn, ROPE_DIM // 2, 1))
        return xn * cosm + partner * sinm

    seq_start = pl.program_id(0) % steps_per_seq == 0

    def conv_silu(y, g):
        prev = jnp.where(seq_start, 0.0, carry_ref[g])
        carry_ref[g] = y[IN_BM - 8:, :]
        x = jnp.concatenate([prev, y], axis=0)
        w = cw_ref[:, g * GROUP:(g + 1) * GROUP]
        t = w[0:1] * x
        for j in range(1, CONV_K):
            t = w[j:j + 1] * x + pltpu.roll(t, 1, 0)
        return _silu(t[8:] + cb_ref[:, g * GROUP:(g + 1) * GROUP])

    def store_chunks_transposed(dst_ref, val):
        tmp_ref[...] = val
        for c in range(IN_BM // C_CH):
            dst_ref[0, c] = tmp_ref[c * C_CH:(c + 1) * C_CH, :].T.astype(BF16)

    def project(wid):
        w_ref, first = (wa_ref, 0) if wid < N_BEFORE_GATES else (wb_ref, N_BEFORE_GATES)
        return _bdot(h, w_ref[:, (wid - first) * GROUP:(wid - first + 1) * GROUP])

    def finish(wid, y):
        if wid == W_BQ:
            tmp_ref[...] = norm_rope(y, gq_ref[...]) * (B_DQK ** -0.5 * LOG2E)
            qt_ref[0] = tmp_ref[...].T.astype(BF16)
        elif wid == W_BK:
            k_ref[...] = norm_rope(y, gk_ref[...]).astype(BF16)
        elif wid == W_BV:
            tmp_ref[...] = y
            vt_ref[0, 0] = tmp_ref[...].T.astype(BF16)
        elif wid == W_CK:
            store_chunks_transposed(ckt_ref, conv_silu(y, 1) * (HD ** -0.5))
        else:
            if wid == W_CQ:
                y = conv_silu(y, 0)
            elif wid == W_AI:
                store_chunks_transposed(avt_ref, y)
            cid = OUT_COL[wid]
            o_ref[:, cid * GROUP:(cid + 1) * GROUP] = y.astype(o_ref.dtype)

    busy = (W_BQ, W_BK, W_CQ, W_CK, W_BV, W_AI)
    plain = [w for w in range(N_MAIN // GROUP) if w not in busy]
    ys = [project(wid) for wid in busy]
    for i, wid in enumerate(busy):
        finish(wid, ys[i])
        for p in plain[i * len(plain) // len(busy):(i + 1) * len(plain) // len(busy)]:
            finish(p, project(p))


def _inproj(x2, g, w_a, w_b, wgt, cosm, sinm, gq, gk, cw, cb, bsz, s):
    assert IN_BM == P_BM and IN_BM % C_CH == 0
    t = x2.shape[0]
    nb = s // P_BM
    ppk = BK // P_BM
    cpb = IN_BM // C_CH
    seg = jnp.arange(GROUP) // B_DQK
    bd = (jnp.where(seg[:, None] == seg[None, :], 1.0 / B_DQK, 0.0)).astype(BF16)
    chunk_t = pl.BlockSpec((1, cpb, GROUP, C_CH), lambda i: (i // nb, i % nb, 0, 0))
    return pl.pallas_call(
        functools.partial(_inproj_body, steps_per_seq=nb),
        grid=(t // IN_BM,),
        in_specs=[pl.BlockSpec((IN_BM, D_MODEL), lambda i: (i, 0)),
                  pl.BlockSpec((1, D_MODEL), lambda i: (0, 0)),
                  pl.BlockSpec(w_a.shape, lambda i: (0, 0), pipeline_mode=pl.Buffered(1)),
                  pl.BlockSpec(w_b.shape, lambda i: (0, 0), pipeline_mode=pl.Buffered(1)),
                  pl.BlockSpec((16, D_MODEL), lambda i: (0, 0)),
                  pl.BlockSpec((IN_BM, LANES), lambda i: (i, 0)),
                  pl.BlockSpec((IN_BM, LANES), lambda i: (i, 0)),
                  pl.BlockSpec((1, GROUP), lambda i: (0, 0)),
                  pl.BlockSpec((1, GROUP), lambda i: (0, 0)),
                  pl.BlockSpec((GROUP, GROUP), lambda i: (0, 0)),
                  pl.BlockSpec((CONV_K, 2 * GROUP), lambda i: (0, 0)),
                  pl.BlockSpec((1, 2 * GROUP), lambda i: (0, 0))],
        out_specs=[pl.BlockSpec((IN_BM, N_OUT), lambda i: (i, 0)),
                   pl.BlockSpec((IN_BM // LANES, 8, LANES), lambda i: (i, 0, 0)),
                   pl.BlockSpec((1, GROUP, P_BM), lambda i: (i // nb, 0, i % nb)),
                   pl.BlockSpec((P_BM, GROUP), lambda i: (i, 0)),
                   pl.BlockSpec((1, 1, GROUP, P_BM),
                                lambda i: (i // nb, (i % nb) // ppk, 0, (i % nb) % ppk)),
                   chunk_t, chunk_t],
        out_shape=[jax.ShapeDtypeStruct((t, N_OUT), BF16),
                   jax.ShapeDtypeStruct((t // LANES, 8, LANES), F32),
                   jax.ShapeDtypeStruct((bsz, GROUP, s), BF16),
                   jax.ShapeDtypeStruct((t, GROUP), BF16),
                   jax.ShapeDtypeStruct((bsz, s // BK, GROUP, BK), BF16),
                   jax.ShapeDtypeStruct((bsz, s // C_CH, GROUP, C_CH), BF16),
                   jax.ShapeDtypeStruct((bsz, s // C_CH, GROUP, C_CH), BF16)],
        scratch_shapes=[pltpu.VMEM((IN_BM, GROUP), F32),
                        pltpu.VMEM((2, 8, GROUP), F32)],
        compiler_params=_cp(("arbitrary",), 56),
        name="in_proj",
    )(x2, g, w_a, w_b, wgt, cosm, sinm, gq, gk, bd, cw, cb)


def _hgrn_gates(rows, af_ref, lb, oml, lf_s, ka_s):
    sg = jax.nn.sigmoid(af_ref[rows, :].astype(F32))
    logf = jnp.log2(jnp.maximum(lb + oml * sg, TINY))
    lf_s[...] = logf
    ka_s[...] = oml * (1.0 - sg)
    sums = jnp.sum(logf.reshape(A_CH // A_SUB, A_SUB, GROUP), axis=1)
    return -jnp.min(sums)


def _hgrn_prep(rows, aq_ref, ai_ref, ag_ref, lf_s, ka_s):
    q = _silu(aq_ref[rows, :].astype(F32)) * (HD ** -0.5)
    v = ai_ref[rows, :].astype(F32)
    gate = _silu(ag_ref[rows, :].astype(F32))
    return q, ka_s[...], v, gate, _cumsum_rows(lf_s[...])


def _hgrn_finish(pre, st_ref, vt_ref, bounded):
    q, ka, v, gate, b = pre
    nsub = A_CH // A_SUB
    half = A_SUB // 2
    row = lax.broadcasted_iota(jnp.int32, (A_CH, HD), 0)
    tsub = lax.broadcasted_iota(jnp.int32, (nsub, half, HD), 1)
    ti = lax.broadcasted_iota(jnp.int32, (A_CH, A_CH), 0)
    si = lax.broadcasted_iota(jnp.int32, (A_CH, A_CH), 1)
    blast = b[A_CH - 1:A_CH, :]
    qdec = q * jnp.exp2(b)
    kdec = ka * jnp.exp2(blast - b)
    sdec = jnp.exp2(blast)
    outs = []
    for h in range(HEADS):
        sl = slice(h * HD, (h + 1) * HD)
        qh, kh, vh, bh = q[:, sl], ka[:, sl], v[:, sl], b[:, sl]
        vhb = vh.astype(BF16)
        st = st_ref[h]
        o = lax.dot_general(qdec[:, sl].astype(BF16), st.astype(BF16), NT, preferred_element_type=F32)
        parts = []
        for i in range(nsub):
            lo = i * A_SUB
            hi = lo + A_SUB if bounded else lo
            if hi == 0:
                parts.append(jnp.zeros((A_SUB, A_CH), F32))
                continue
            bi = bh[lo - 1:lo, :] if i else jnp.zeros((1, HD), F32)
            qi = (qh[lo:lo + A_SUB] * jnp.exp2(bh[lo:lo + A_SUB] - bi)).astype(BF16)
            kp = (kh[:hi] * jnp.exp2(bi - bh[:hi])).astype(BF16)
            part = lax.dot_general(qi, kp, NT, preferred_element_type=F32)
            if hi < A_CH:
                part = jnp.concatenate([part, jnp.zeros((A_SUB, A_CH - hi), F32)], axis=1)
            parts.append(part)
        amat = jnp.concatenate(parts, axis=0)
        if bounded:
            amat = jnp.where(si <= ti, amat, 0.0)
        o = o + _bdot(amat.astype(BF16), vhb)
        if not bounded:
            q3 = qh.reshape(nsub, A_SUB, HD)
            k3 = kh.reshape(nsub, A_SUB, HD)
            b3 = bh.reshape(nsub, A_SUB, HD)
            v3 = vh.reshape(nsub, A_SUB, HD)
            q_lo, q_hi = q3[:, :half], q3[:, half:]
            b_lo, b_hi = b3[:, :half], b3[:, half:]
            acc_lo = jnp.zeros((nsub, half, HD), F32)
            acc_hi = jnp.zeros((nsub, half, HD), F32)
            for s in range(A_SUB):
                bs, ks, vs = b3[:, s:s + 1, :], k3[:, s:s + 1, :], v3[:, s:s + 1, :]
                if s < half:
                    e_lo = jnp.exp2(jnp.where(tsub >= s, b_lo - bs, NEG))
                    acc_lo = acc_lo + jnp.sum(q_lo * ks * e_lo, axis=-1, keepdims=True) * vs
                    e_hi = jnp.exp2(b_hi - bs)
                else:
                    e_hi = jnp.exp2(jnp.where(tsub >= s - half, b_hi - bs, NEG))
                acc_hi = acc_hi + jnp.sum(q_hi * ks * e_hi, axis=-1, keepdims=True) * vs
            o = o + jnp.concatenate([acc_lo, acc_hi], axis=1).reshape(A_CH, HD)
        st_ref[h] = st * sdec[:, sl] + _bdot(vt_ref[sl, :], kdec[:, sl].astype(BF16))
        outs.append(o)
    return outs


def _mlstm_gate_rows(grow_ref, gbc, triu_ref):
    nc = grow_ref.shape[0]
    g = grow_ref[...] + gbc
    srow = lax.broadcasted_iota(jnp.int32, g.shape, 1)
    g = jnp.where(srow < HEADS, g, _log_sigmoid(g)) * LOG2E
    cum = jnp.dot(g.reshape(nc * 8, LANES), triu_ref[...], precision=HI,
                  preferred_element_type=F32).reshape(nc, 8, LANES)
    b = cum[:, HEADS:, :]
    ab = jnp.concatenate([g[:, :HEADS, :] - b, b], axis=1)
    cm = ab.reshape(nc * 8, LANES)
    lane = lax.broadcasted_iota(jnp.int32, cm.shape, 1)
    d = 1
    while d < LANES:
        cm = jnp.maximum(cm, jnp.where(lane >= d, pltpu.roll(cm, d, 1), NEG))
        d *= 2
    return ab, cm.reshape(nc, 8, LANES)


def _mlstm_vectors(ab_ref, cm_ref, m_s):
    ab = ab_ref[...]
    a4, b4 = ab[:HEADS], ab[HEADS:]
    m_prev = m_s[0:HEADS, :]
    mx4 = jnp.maximum(cm_ref[0:HEADS, :], m_prev)
    mt4 = b4 + mx4
    mx_last = jnp.broadcast_to(mx4[:, C_CH - 1:C_CH], (HEADS, LANES))
    wg4 = jnp.exp2(a4 - mx_last)
    dec4 = jnp.exp2(m_prev - mx_last)
    m_s[0:HEADS, :] = jnp.broadcast_to(mt4[:, C_CH - 1:C_CH], (HEADS, LANES))
    cols = [(_colmat(mx4[h:h + 1]), _colmat(mt4[h:h + 1])) for h in range(HEADS)]
    return a4, wg4, dec4, m_prev, cols


def _mlstm_chunk(rows, per_batch, vecs):
    ri = lax.broadcasted_iota(jnp.int32, (C_CH, C_CH), 0)
    ci = lax.broadcasted_iota(jnp.int32, (C_CH, C_CH), 1)
    causal = ri >= ci
    ones_blk = jnp.ones((C_CH, HD), BF16)
    hcs = []
    for (cv_ref, co_ref, cq_ref, kt_ref, ct_s), (a4, wg4, dec4, m_prev, cols) in zip(per_batch, vecs):
        for h in range(HEADS):
            sl = slice(h * HD, (h + 1) * HD)
            row = slice(h, h + 1)
            mxm, mtm = cols[h]
            ct = ct_s[h]
            qb = cq_ref[rows, sl]
            qf = qb.astype(F32)
            ktb = kt_ref[sl, :]
            kt = ktb.astype(F32)
            vaug = jnp.concatenate([cv_ref[rows, sl], ones_blk], axis=1)

            w = jnp.exp2(jnp.where(causal, a4[row] - mxm, NEG))
            sc = _bdot(qb, ktb)
            lhs = jnp.concatenate([(sc * w).astype(BF16),
                                   (qf * jnp.exp2(m_prev[row] - mxm)).astype(BF16)], axis=1)
            tot = _bdot(lhs, jnp.concatenate([vaug, ct.astype(BF16)], axis=0))
            hout = tot[:, :HD] / jnp.maximum(jnp.abs(tot[:, HD:]), jnp.exp2(-mtm))

            dec2 = jnp.concatenate([dec4[row], dec4[row]], axis=1)
            ct_s[h] = dec2 * ct + _bdot((kt * wg4[row]).astype(BF16), vaug)
            hcs.append(jax.nn.sigmoid(co_ref[rows, sl].astype(F32)) * hout)
    return hcs


def _rec_body(aq_ref, af_ref, ai_ref, ag_ref, cq_ref, cv_ref, co_ref, cg_ref, kt_ref, vt_ref, grow_ref,
              lb_ref, ang_ref, gbc_ref, cng_ref, triu_ref, mean_ref,
              ya_ref, yc_ref, st_s, ab_s, cm_s, ct_s, m_s, lf_s, ka_s):
    bsz = aq_ref.shape[0]
    nchunk = R_BLK // C_CH

    @pl.when(pl.program_id(0) == 0)
    def _():
        st_s[...] = jnp.zeros_like(st_s)
        ct_s[...] = jnp.zeros_like(ct_s)
        m_s[...] = jnp.zeros_like(m_s)

    gbc = gbc_ref[...]
    for bi in range(bsz):
        ab_s[bi], cm_s[bi] = _mlstm_gate_rows(grow_ref.at[bi], gbc, triu_ref)

    lb = lb_ref[...]
    oml = 1.0 - lb
    ang = ang_ref[...]
    cng = cng_ref[...]

    def chunk(c, carry):
        rows = pl.ds(pl.multiple_of(c * C_CH, C_CH), C_CH)
        spans = [_hgrn_gates(rows, af_ref.at[bi], lb, oml, lf_s.at[bi], ka_s.at[bi]) for bi in range(bsz)]
        bounded = functools.reduce(jnp.maximum, spans) <= A_MAX_SUB_LOG2_DECAY

        def rest(is_bounded):
            vecs = [_mlstm_vectors(ab_s.at[bi, c], cm_s.at[bi, c], m_s.at[bi]) for bi in range(bsz)]
            pres = [_hgrn_prep(rows, aq_ref.at[bi], ai_ref.at[bi], ag_ref.at[bi], lf_s.at[bi], ka_s.at[bi])
                    for bi in range(bsz)]
            outs = [o for bi in range(bsz)
                    for o in _hgrn_finish(pres[bi], st_s.at[bi], vt_ref.at[bi, c], is_bounded)]
            outs = _unit_rms_heads(outs, mean_ref)
            per_batch = [(cv_ref.at[bi], co_ref.at[bi], cq_ref.at[bi], kt_ref.at[bi, c], ct_s.at[bi])
                         for bi in range(bsz)]
            hcs = _unit_rms_heads(_mlstm_chunk(rows, per_batch, vecs), mean_ref)
            for i in range(bsz * HEADS):
                bi, h = divmod(i, HEADS)
                sl = slice(h * HD, (h + 1) * HD)
                ya_ref[bi, rows, sl] = (outs[i] * ang * pres[bi][3][:, sl]).astype(ya_ref.dtype)
                yc_ref[bi, rows, sl] = (hcs[i] * cng * _silu(cg_ref[bi, rows, sl].astype(F32))
                                        ).astype(yc_ref.dtype)

        pl.when(bounded)(functools.partial(rest, True))
        pl.when(jnp.logical_not(bounded))(functools.partial(rest, False))
        return carry

    lax.fori_loop(0, nchunk, chunk, 0)


def _recurrent(proj, ckt, avt, grow, lb, ang, gbc, cng, bsz, s):
    assert A_CH == C_CH
    proj3 = proj.reshape(bsz, s, N_OUT)
    nchunk = R_BLK // C_CH
    col = lambda cid: pl.BlockSpec((bsz, R_BLK, GROUP), lambda t, cid=cid: (0, t, cid))
    full = lambda shp: pl.BlockSpec(shp, lambda t: (0,) * len(shp))
    chunk_t = pl.BlockSpec((bsz, nchunk, GROUP, C_CH), lambda t: (0, t, 0, 0))
    triu = jnp.triu(jnp.ones((C_CH, C_CH), F32))
    mean_mat = jnp.full((HD, HD), 1.0 / HD, BF16)
    out = pl.BlockSpec((bsz, R_BLK, GROUP), lambda t: (0, t, 0))
    ya, yc = pl.pallas_call(
        _rec_body,
        grid=(s // R_BLK,),
        in_specs=[col(C_AQ), col(C_AF), col(C_AI), col(C_AG),
                  col(C_CQ), col(C_CV), col(C_CO), col(C_CG), chunk_t, chunk_t,
                  pl.BlockSpec((bsz, nchunk, 8, LANES), lambda t: (0, t, 0, 0)),
                  full((1, GROUP)), full((1, HD)),
                  full((8, 1)), full((1, HD)),
                  full((C_CH, C_CH)), full((HD, HD))],
        out_specs=[out, out],
        out_shape=[jax.ShapeDtypeStruct((bsz, s, GROUP), BF16)] * 2,
        scratch_shapes=[pltpu.VMEM((bsz, HEADS, HD, HD), F32),
                        pltpu.VMEM((bsz, nchunk, 8, LANES), F32), pltpu.VMEM((bsz, nchunk, 8, LANES), F32),
                        pltpu.VMEM((bsz, HEADS, HD, 2 * HD), F32), pltpu.VMEM((bsz, 8, LANES), F32),
                        pltpu.VMEM((bsz, A_CH, GROUP), F32), pltpu.VMEM((bsz, A_CH, GROUP), F32)],
        compiler_params=_cp(("arbitrary",), 48),
        name="hgrn2_mlstm",
    )(*([proj3] * 8), ckt, avt, grow.reshape(bsz, s // LANES, 8, LANES),
      lb, ang, gbc, cng, triu, mean_mat)
    return ya.reshape(bsz * s, GROUP), yc.reshape(bsz * s, GROUP)


def _flash_body(qt_ref, k_ref, vt_ref, bg_ref, lam_ref, sg_ref, y_ref,
                qz_s, m_s, l_s, acc_s, *, lam_init, online_max):
    qi = pl.program_id(2)
    qt = qt_ref[0]
    rowi = lax.broadcasted_iota(jnp.int32, (HD, BQ), 0)
    zero = jnp.zeros_like(qt)
    qz_s[0] = jnp.where(rowi < B_DQK, qt, zero)
    qz_s[1] = jnp.where(rowi >= B_DQK, qt, zero)
    m_s[...] = jnp.full_like(m_s, NEG)
    l_s[...] = jnp.zeros_like(l_s)
    acc_s[...] = jnp.zeros_like(acc_s)

    def step(kblk, vblk, q0, mask):
        qs = slice(q0, BQ)
        scores = [_bdot(kblk, qz_s[c, :, qs]) for c in range(2)]
        for c in range(2):
            s = scores[c]
            if mask is not None:
                s = jnp.where(mask, s, NEG)
            if online_max:
                m_old = m_s[c, :, qs]
                m_new = jnp.maximum(m_old, jnp.max(s, axis=0, keepdims=True))
                alpha = jnp.exp2(m_old - m_new)
                p = jnp.exp2(s - m_new)
                l_s[c, :, qs] = alpha * l_s[c, :, qs] + jnp.sum(p, axis=0, keepdims=True)
                acc_s[c, :, qs] = alpha * acc_s[c, :, qs] + _bdot(vblk, p.astype(BF16))
                m_s[c, :, qs] = m_new
            else:
                p = jnp.exp2(s)
                l_s[c, :, qs] = l_s[c, :, qs] + jnp.sum(p, axis=0, keepdims=True)
                acc_s[c, :, qs] = acc_s[c, :, qs] + _bdot(vblk, p.astype(BF16))

    def body(kb, carry):
        r0 = pl.multiple_of(kb * BK, BK)
        step(k_ref[pl.ds(r0, BK), :], vt_ref[0, kb], 0, None)
        return carry

    kpq = BQ // BK
    lax.fori_loop(0, qi * kpq, body, 0)
    for d in range(BQ // BKD):
        q0 = d * BKD
        kidx = lax.broadcasted_iota(jnp.int32, (BKD, BQ - q0), 0)
        qidx = lax.broadcasted_iota(jnp.int32, (BKD, BQ - q0), 1)
        rows = pl.ds(pl.multiple_of(qi * BQ + q0, BKD), BKD)
        lo = q0 % BK
        vblk = vt_ref[0, qi * kpq + q0 // BK][:, lo:lo + BKD]
        step(k_ref[rows, :], vblk, q0, kidx <= qidx)

    lp = lam_ref[...]
    lam = (jnp.exp(jnp.sum(lp[0:1] * lp[1:2], axis=1, keepdims=True))
           - jnp.exp(jnp.sum(lp[2:3] * lp[3:4], axis=1, keepdims=True)) + lam_init)
    ot = acc_s[0] / l_s[0] - lam * (acc_s[1] / l_s[1])
    o = ot.T
    y = _rms(o, sg_ref[...]) * (1.0 - lam_init) * _silu(bg_ref[...].astype(F32))
    y_ref[...] = y.astype(y_ref.dtype)


def _flash_call(qt, kk, vt, proj, lam_p, sg, *, bsz, s, lam_init, online_max):
    nq = s // BQ
    return pl.pallas_call(
        functools.partial(_flash_body, lam_init=lam_init, online_max=online_max),
        grid=(bsz, HEADS, nq),
        in_specs=[pl.BlockSpec((1, HD, BQ), lambda b, h, q: (b, h, q)),
                  pl.BlockSpec((s, HD), lambda b, h, q: (b, h)),
                  pl.BlockSpec((1, s // BK, HD, BK), lambda b, h, q: (b, 0, h, 0)),
                  pl.BlockSpec((BQ, HD), lambda b, h, q: (b * nq + q, C_BG * HEADS + h)),
                  pl.BlockSpec((4, B_DQK), lambda b, h, q: (0, 0)),
                  pl.BlockSpec((1, HD), lambda b, h, q: (0, 0))],
        out_specs=pl.BlockSpec((BQ, HD), lambda b, h, q: (b * nq + q, h)),
        out_shape=jax.ShapeDtypeStruct((bsz * s, GROUP), BF16),
        scratch_shapes=[pltpu.VMEM((2, HD, BQ), BF16), pltpu.VMEM((2, 1, BQ), F32),
                        pltpu.VMEM((2, 1, BQ), F32), pltpu.VMEM((2, HD, BQ), F32)],
        compiler_params=_cp(("arbitrary", "arbitrary", "arbitrary"), 48),
        name="diff_flash_online" if online_max else "diff_flash",
    )(qt, kk, vt, proj, lam_p, sg)


def _flash(qt, kk, vt, proj, lam_p, sg, qk_g, bsz, s, lam_init):
    bound = (B_DQK ** 0.5) * LOG2E * jnp.max(jnp.abs(qk_g[0])) * jnp.max(jnp.abs(qk_g[1]))
    args = (qt, kk, vt, proj, lam_p, sg)
    call = functools.partial(_flash_call, bsz=bsz, s=s, lam_init=lam_init)
    return lax.cond(bound <= MAX_UNSHIFTED_LOG2_SCORE - math.log2(s),
                    functools.partial(call, online_max=False),
                    functools.partial(call, online_max=True), *args)


def _memkv_body(mem_ref, g_ref, w_ref, kg_ref, km_ref, vm_ref):
    mn = _rms(mem_ref[0], g_ref[...]).astype(BF16)
    kv = _bdot(mn, w_ref[...])
    for h in range(HEADS):
        sl = slice(h * HD, (h + 1) * HD)
        km_ref[0, :, sl] = _rms(kv[:, sl], kg_ref[...]).astype(BF16)
    vm_ref[0] = kv[:, GROUP:].astype(BF16)


def _memkv(mem, g, w, kg):
    bsz = mem.shape[0]
    return pl.pallas_call(
        _memkv_body,
        grid=(bsz,),
        in_specs=[pl.BlockSpec((1, N_MEM, D_MODEL), lambda b: (b, 0, 0)),
                  pl.BlockSpec((1, D_MODEL), lambda b: (0, 0)),
                  pl.BlockSpec((D_MODEL, 2 * GROUP), lambda b: (0, 0)),
                  pl.BlockSpec((1, HD), lambda b: (0, 0))],
        out_specs=[pl.BlockSpec((1, N_MEM, GROUP), lambda b: (b, 0, 0))] * 2,
        out_shape=[jax.ShapeDtypeStruct((bsz, N_MEM, GROUP), BF16)] * 2,
        compiler_params=_cp(("arbitrary",), 32),
        name="mem_kv",
    )(mem, g, w, kg)


def _merge_body(x_ref, ya_ref, yb_ref, yc_ref, xq_ref, xg_ref, km_ref, vm_ref, qg_ref, w_ref, o_ref):
    acc = x_ref[...]
    acc = acc + _bdot(ya_ref[...], w_ref[0:GROUP, :])
    acc = acc + _bdot(yb_ref[...], w_ref[GROUP:2 * GROUP, :])
    acc = acc + _bdot(yc_ref[...], w_ref[2 * GROUP:3 * GROUP, :])
    qg = qg_ref[...]
    yx = []
    for h in range(HEADS):
        sl = slice(h * HD, (h + 1) * HD)
        qn = (_rms(xq_ref[:, sl].astype(F32), qg) * (HD ** -0.5)).astype(BF16)
        s = lax.dot_general(qn, km_ref[0, :, sl], NT, preferred_element_type=F32)
        p = jnp.exp(s - jnp.max(s, axis=-1, keepdims=True))
        l = jnp.sum(p, axis=-1, keepdims=True)
        ox = _bdot(p.astype(BF16), vm_ref[0, :, sl]) / l
        yx.append((ox * _silu(xg_ref[:, sl].astype(F32))).astype(BF16))
    o_ref[...] = acc + _bdot(jnp.concatenate(yx, axis=1), w_ref[3 * GROUP:4 * GROUP, :])


def _merge(x2, ya, yb, yc, proj, km, vm, qg, w_out, bsz, s):
    nb = s // M_BM
    t = bsz * s
    blk = lambda: pl.BlockSpec((M_BM, GROUP), lambda i: (i, 0))
    col = lambda cid: pl.BlockSpec((M_BM, GROUP), lambda i, cid=cid: (i, cid))
    return pl.pallas_call(
        _merge_body,
        grid=(t // M_BM,),
        in_specs=[pl.BlockSpec((M_BM, D_MODEL), lambda i: (i, 0)),
                  blk(), blk(), blk(), col(C_XQ), col(C_XG),
                  pl.BlockSpec((1, N_MEM, GROUP), lambda i: (i // nb, 0, 0)),
                  pl.BlockSpec((1, N_MEM, GROUP), lambda i: (i // nb, 0, 0)),
                  pl.BlockSpec((1, HD), lambda i: (0, 0)),
                  pl.BlockSpec((4 * GROUP, D_MODEL), lambda i: (0, 0))],
        out_specs=pl.BlockSpec((M_BM, D_MODEL), lambda i: (i, 0)),
        out_shape=jax.ShapeDtypeStruct((t, D_MODEL), F32),
        compiler_params=_cp(("arbitrary",), 48),
        name="xattn_out_proj",
    )(x2, ya, yb, yc, proj, proj, km, vm, qg, w_out)


def _layer(x2, mem, cosm, sinm, layer_idx, lb, norm_g, w_in, mlstm_gate_b, hgrn_norm_g,
           diff_qk_norm_g, diff_lambda, diff_subln_g, mlstm_conv_w, mlstm_conv_b,
           mlstm_norm_g, mem_norm_g, w_mem_kv, xattn_qk_norm_g, w_out, bsz, s):
    g12 = N_BEFORE_GATES * GROUP
    w_a = w_in[:, :g12].astype(BF16)
    w_b = w_in[:, g12 + 2 * HEADS:].astype(BF16)
    wgt = jnp.pad(w_in[:, g12:g12 + 2 * HEADS].T, ((0, 8), (0, 0))).astype(BF16)
    gq = jnp.tile(diff_qk_norm_g[0], GROUP // B_DQK)[None, :]
    gk = jnp.tile(diff_qk_norm_g[1], GROUP // B_DQK)[None, :]
    proj, grow, qt, kk, vt, ckt, avt = _inproj(x2, norm_g[None, :], w_a, w_b, wgt, cosm, sinm, gq, gk,
                                               mlstm_conv_w, mlstm_conv_b[None, :], bsz, s)

    ya, yc = _recurrent(proj, ckt, avt, grow, lb[None, :], hgrn_norm_g[None, :],
                        mlstm_gate_b[:, None], mlstm_norm_g[None, :], bsz, s)

    lam_init = 0.8 - 0.6 * math.exp(-0.3 * layer_idx)
    yb = _flash(qt, kk, vt, proj, diff_lambda, diff_subln_g[None, :], diff_qk_norm_g, bsz, s, lam_init)

    km, vm = _memkv(mem, mem_norm_g[None, :], w_mem_kv.astype(BF16), xattn_qk_norm_g[1][None, :])
    return _merge(x2, ya, yb, yc, proj, km, vm, xattn_qk_norm_g[0][None, :],
                  w_out.astype(BF16), bsz, s)


def kernel(x, mem, positions, norm_g, w_in, mlstm_gate_b, hgrn_lb_logits, hgrn_norm_g,
           diff_qk_norm_g, diff_lambda, diff_subln_g, mlstm_conv_w, mlstm_conv_b,
           mlstm_norm_g, mem_norm_g, w_mem_kv, xattn_qk_norm_g, w_out):
    bsz, s, d = x.shape
    depth = norm_g.shape[0]
    assert d == D_MODEL and mem.shape[1] == N_MEM
    assert all(s % blk == 0 for blk in (IN_BM, R_BLK, P_BM, BQ, BK, M_BM)) and (bsz * s) % T_BM == 0
    cosm, sinm = _rope_tables(positions)
    sm = jax.nn.softmax(hgrn_lb_logits.astype(F32), axis=0)
    lower_bounds = jnp.cumsum(sm, axis=0) - sm[0]
    x2 = x.reshape(bsz * s, d)
    for l in range(depth):
        x2 = _layer(x2, mem, cosm, sinm, l, lower_bounds[l], norm_g[l], w_in[l], mlstm_gate_b[l],
                    hgrn_norm_g[l], diff_qk_norm_g[l], diff_lambda[l], diff_subln_g[l],
                    mlstm_conv_w[l], mlstm_conv_b[l], mlstm_norm_g[l], mem_norm_g[l],
                    w_mem_kv[l], xattn_qk_norm_g[l], w_out[l], bsz, s)
    return x2.reshape(bsz, s, d)
```

```python
import functools
import math

import jax
import jax.numpy as jnp
from jax import lax
from jax.experimental import pallas as pl
from jax.experimental.pallas import tpu as pltpu

F32 = jnp.float32
BF16 = jnp.bfloat16
HI = lax.Precision.HIGHEST

D_MODEL = 1024
N_MEM = 256
GROUP = 512
HEADS = 4
HD = 128
B_DQK = 64
ROPE_DIM = 16
ROPE_THETA = 500000.0
CONV_K = 4
EPS = 1e-6
NEG = -1e30
TINY = 1e-30
LOG2E = math.log2(math.e)
MAX_UNSHIFTED_LOG2_SCORE = 100.0

LANES = 128
N_MAIN = 15 * GROUP

(W_AQ, W_AF, W_AI, W_AG, W_BQ, W_BK, W_BV, W_BG,
 W_CQ, W_CK, W_CV, W_CO, W_CG, W_XQ, W_XG) = range(15)
(C_AQ, C_AF, C_AI, C_AG, C_BG, C_CQ, C_CV, C_CO, C_CG, C_XQ, C_XG) = range(11)
OUT_COL = {W_AQ: C_AQ, W_AF: C_AF, W_AI: C_AI, W_AG: C_AG, W_BG: C_BG, W_CQ: C_CQ,
           W_CV: C_CV, W_CO: C_CO, W_CG: C_CG, W_XQ: C_XQ, W_XG: C_XG}
N_OUT = len(OUT_COL) * GROUP

IN_BM, IN_BN = 512, 1536
R_BLK = 512
A_CH, A_SUB = 128, 16
A_MAX_SUB_LOG2_DECAY = 60.0
C_CH = 128
P_BM = 512
BQ = 2048
BK = 1024
BKD = 512
M_BM = 512
T_BM = 1024

NT = (((1,), (1,)), ((), ()))
TN = (((0,), (0,)), ((), ()))


def _cp(sem, vmem_mib):
    return pltpu.CompilerParams(dimension_semantics=sem, vmem_limit_bytes=vmem_mib * 2 ** 20)


def _silu(x):
    return x * jax.nn.sigmoid(x)


def _log_sigmoid(x):
    return jnp.minimum(x, 0.0) - jnp.log1p(jnp.exp(-jnp.abs(x)))


def _rms(x, g):
    return x * lax.rsqrt(jnp.mean(x * x, axis=-1, keepdims=True) + EPS) * g


def _bdot(a, b):
    return jnp.dot(a, b, preferred_element_type=F32)


def _unit_rms_heads(xs, mean_ref):
    n = xs[0].shape[0]
    x = jnp.concatenate(xs, axis=0)
    y = x * lax.rsqrt(_bdot((x * x).astype(BF16), mean_ref[...]) + EPS)
    return [y[i * n:(i + 1) * n] for i in range(len(xs))]


def _cumsum_rows(x):
    row = lax.broadcasted_iota(jnp.int32, x.shape, 0)
    d = 1
    while d < x.shape[0]:
        x = x + jnp.where(row >= d, pltpu.roll(x, d, 0), 0.0)
        d *= 2
    return x


def _colmat(r):
    return jnp.broadcast_to(r, (r.shape[1], r.shape[1])).T


def _rope_body(pos_ref, invf_ref, cos_ref, sin_ref):
    ang = pos_ref[...].astype(F32) * invf_ref[...]
    lane = lax.broadcasted_iota(jnp.int32, ang.shape, 1) % B_DQK
    c, s = jnp.cos(ang), jnp.sin(ang)
    cos_ref[...] = jnp.where(lane < ROPE_DIM, c, 1.0)
    sin_ref[...] = jnp.where(lane < ROPE_DIM // 2, -s, jnp.where(lane < ROPE_DIM, s, 0.0))


def _rope_tables(positions):
    t = positions.size
    inv_freq = ROPE_THETA ** (-jnp.arange(0, ROPE_DIM, 2, dtype=F32) / ROPE_DIM)
    invf = jnp.tile(inv_freq, LANES // (ROPE_DIM // 2))[None, :]
    return pl.pallas_call(
        _rope_body,
        grid=(t // T_BM,),
        in_specs=[pl.BlockSpec((T_BM, 1), lambda i: (i, 0)),
                  pl.BlockSpec((1, LANES), lambda i: (0, 0))],
        out_specs=[pl.BlockSpec((T_BM, LANES), lambda i: (i, 0))] * 2,
        out_shape=[jax.ShapeDtypeStruct((t, LANES), F32)] * 2,
        compiler_params=_cp(("arbitrary",), 32),
        name="rope_tables",
    )(positions.reshape(t, 1), invf)


def _inproj_body(x_ref, g_ref, w_ref, wgt_ref, cos_ref, sin_ref, gq_ref, gk_ref, bd_ref, cw_ref, cb_ref,
                 o_ref, gr_ref, qt_ref, k_ref, vt_ref, ckt_ref, avt_ref, tmp_ref, carry_ref,
                 *, steps_per_seq):
    @pl.when(pl.program_id(0) == 0)
    def _():
        carry_ref[...] = jnp.zeros_like(carry_ref)

    half = IN_BM // 2
    w_of = lambda wid: w_ref[:, wid * GROUP:(wid + 1) * GROUP]
    busy = (W_BQ, W_BK, W_CQ, W_CK, W_BV, W_AI)

    def norm_half(r):
        hh = _rms(x_ref[r * half:(r + 1) * half, :], g_ref[...]).astype(BF16)
        gr = lax.dot_general(wgt_ref[...], hh, NT, preferred_element_type=F32)
        for j in range(half // LANES):
            gr_ref[r * (half // LANES) + j] = gr[:8, j * LANES:(j + 1) * LANES]
        return hh

    h_top = norm_half(0)
    y_top = {wid: _bdot(h_top, w_of(wid)) for wid in busy}
    h_bot = norm_half(1)

    reps = GROUP // LANES
    cosm = jnp.concatenate([cos_ref[...]] * reps, axis=1)
    sinm = jnp.concatenate([sin_ref[...]] * reps, axis=1)
    lane = lax.broadcasted_iota(jnp.int32, (IN_BM, GROUP), 1)
    first_half = (lane % ROPE_DIM) < ROPE_DIM // 2

    def norm_rope(x, gain):
        ms = _bdot((x * x).astype(BF16), bd_ref[...])
        xn = x * lax.rsqrt(ms + EPS) * gain
        partner = jnp.where(first_half,
                            pltpu.roll(xn, GROUP - ROPE_DIM // 2, 1),
                            pltpu.roll(xn, ROPE_DIM // 2, 1))
        return xn * cosm + partner * sinm

    seq_start = pl.program_id(0) % steps_per_seq == 0

    def conv_silu(y, g):
        prev = jnp.where(seq_start, 0.0, carry_ref[g])
        carry_ref[g] = y[IN_BM - 8:, :]
        x = jnp.concatenate([prev, y], axis=0)
        w = cw_ref[:, g * GROUP:(g + 1) * GROUP]
        t = w[0:1] * x
        for j in range(1, CONV_K):
            t = w[j:j + 1] * x + pltpu.roll(t, 1, 0)
        return _silu(t[8:] + cb_ref[:, g * GROUP:(g + 1) * GROUP])

    def store_chunks_transposed(dst_ref, val):
        tmp_ref[...] = val
        for c in range(IN_BM // C_CH):
            dst_ref[0, c] = tmp_ref[c * C_CH:(c + 1) * C_CH, :].T.astype(BF16)

    def project(wid):
        top = y_top[wid] if wid in y_top else _bdot(h_top, w_of(wid))
        return jnp.concatenate([top, _bdot(h_bot, w_of(wid))], axis=0)

    def finish(wid, y):
        if wid == W_BQ:
            tmp_ref[...] = norm_rope(y, gq_ref[...]) * (B_DQK ** -0.5 * LOG2E)
            qt_ref[0] = tmp_ref[...].T.astype(BF16)
        elif wid == W_BK:
            k_ref[...] = norm_rope(y, gk_ref[...]).astype(BF16)
        elif wid == W_BV:
            tmp_ref[...] = y
            vt_ref[0, 0] = tmp_ref[...].T.astype(BF16)
        elif wid == W_CK:
            store_chunks_transposed(ckt_ref, conv_silu(y, 1) * (HD ** -0.5))
        else:
            if wid == W_CQ:
                y = conv_silu(y, 0)
            elif wid == W_AI:
                store_chunks_transposed(avt_ref, y)
            cid = OUT_COL[wid]
            o_ref[:, cid * GROUP:(cid + 1) * GROUP] = y.astype(o_ref.dtype)

    plain = [w for w in range(N_MAIN // GROUP) if w not in busy]
    ys = [project(wid) for wid in busy]
    for i, wid in enumerate(busy):
        finish(wid, ys[i])
        for p in plain[i * len(plain) // len(busy):(i + 1) * len(plain) // len(busy)]:
            finish(p, project(p))


def _inproj(x2, g, w_main, wgt, cosm, sinm, gq, gk, cw, cb, bsz, s):
    assert IN_BM == P_BM and IN_BM % C_CH == 0
    t = x2.shape[0]
    nb = s // P_BM
    ppk = BK // P_BM
    cpb = IN_BM // C_CH
    seg = jnp.arange(GROUP) // B_DQK
    bd = (jnp.where(seg[:, None] == seg[None, :], 1.0 / B_DQK, 0.0)).astype(BF16)
    chunk_t = pl.BlockSpec((1, cpb, GROUP, C_CH), lambda i: (i // nb, i % nb, 0, 0))
    return pl.pallas_call(
        functools.partial(_inproj_body, steps_per_seq=nb),
        grid=(t // IN_BM,),
        in_specs=[pl.BlockSpec((IN_BM, D_MODEL), lambda i: (i, 0)),
                  pl.BlockSpec((1, D_MODEL), lambda i: (0, 0)),
                  pl.BlockSpec((D_MODEL, N_MAIN), lambda i: (0, 0), pipeline_mode=pl.Buffered(1)),
                  pl.BlockSpec((16, D_MODEL), lambda i: (0, 0)),
                  pl.BlockSpec((IN_BM, LANES), lambda i: (i, 0)),
                  pl.BlockSpec((IN_BM, LANES), lambda i: (i, 0)),
                  pl.BlockSpec((1, GROUP), lambda i: (0, 0)),
                  pl.BlockSpec((1, GROUP), lambda i: (0, 0)),
                  pl.BlockSpec((GROUP, GROUP), lambda i: (0, 0)),
                  pl.BlockSpec((CONV_K, 2 * GROUP), lambda i: (0, 0)),
                  pl.BlockSpec((1, 2 * GROUP), lambda i: (0, 0))],
        out_specs=[pl.BlockSpec((IN_BM, N_OUT), lambda i: (i, 0)),
                   pl.BlockSpec((IN_BM // LANES, 8, LANES), lambda i: (i, 0, 0)),
                   pl.BlockSpec((1, GROUP, P_BM), lambda i: (i // nb, 0, i % nb)),
                   pl.BlockSpec((P_BM, GROUP), lambda i: (i, 0)),
                   pl.BlockSpec((1, 1, GROUP, P_BM),
                                lambda i: (i // nb, (i % nb) // ppk, 0, (i % nb) % ppk)),
                   chunk_t, chunk_t],
        out_shape=[jax.ShapeDtypeStruct((t, N_OUT), BF16),
                   jax.ShapeDtypeStruct((t // LANES, 8, LANES), F32),
                   jax.ShapeDtypeStruct((bsz, GROUP, s), BF16),
                   jax.ShapeDtypeStruct((t, GROUP), BF16),
                   jax.ShapeDtypeStruct((bsz, s // BK, GROUP, BK), BF16),
                   jax.ShapeDtypeStruct((bsz, s // C_CH, GROUP, C_CH), BF16),
                   jax.ShapeDtypeStruct((bsz, s // C_CH, GROUP, C_CH), BF16)],
        scratch_shapes=[pltpu.VMEM((IN_BM, GROUP), F32),
                        pltpu.VMEM((2, 8, GROUP), F32)],
        compiler_params=_cp(("arbitrary",), 56),
        name="in_proj",
    )(x2, g, w_main, wgt, cosm, sinm, gq, gk, bd, cw, cb)


def _hgrn_gates(rows, af_ref, lb, oml, lf_s, ka_s):
    sg = jax.nn.sigmoid(af_ref[rows, :].astype(F32))
    logf = jnp.log2(jnp.maximum(lb + oml * sg, TINY))
    lf_s[...] = logf
    ka_s[...] = oml * (1.0 - sg)
    sums = jnp.sum(logf.reshape(A_CH // A_SUB, A_SUB, GROUP), axis=1)
    return -jnp.min(sums)


def _hgrn_prep(rows, aq_ref, ai_ref, ag_ref, lf_s, ka_s):
    q = _silu(aq_ref[rows, :].astype(F32)) * (HD ** -0.5)
    v = ai_ref[rows, :].astype(F32)
    gate = _silu(ag_ref[rows, :].astype(F32))
    return q, ka_s[...], v, gate, _cumsum_rows(lf_s[...])


def _hgrn_finish(pre, st_ref, vt_ref, bounded):
    q, ka, v, gate, b = pre
    nsub = A_CH // A_SUB
    half = A_SUB // 2
    row = lax.broadcasted_iota(jnp.int32, (A_CH, HD), 0)
    tsub = lax.broadcasted_iota(jnp.int32, (nsub, half, HD), 1)
    ti = lax.broadcasted_iota(jnp.int32, (A_CH, A_CH), 0)
    si = lax.broadcasted_iota(jnp.int32, (A_CH, A_CH), 1)
    blast = b[A_CH - 1:A_CH, :]
    qdec = q * jnp.exp2(b)
    kdec = ka * jnp.exp2(blast - b)
    sdec = jnp.exp2(blast)
    outs = []
    for h in range(HEADS):
        sl = slice(h * HD, (h + 1) * HD)
        qh, kh, vh, bh = q[:, sl], ka[:, sl], v[:, sl], b[:, sl]
        vhb = vh.astype(BF16)
        st = st_ref[h]
        o = lax.dot_general(qdec[:, sl].astype(BF16), st.astype(BF16), NT, preferred_element_type=F32)
        parts = []
        for i in range(nsub):
            lo = i * A_SUB
            hi = lo + A_SUB if bounded else lo
            if hi == 0:
                parts.append(jnp.zeros((A_SUB, A_CH), F32))
                continue
            bi = bh[lo - 1:lo, :] if i else jnp.zeros((1, HD), F32)
            qi = (qh[lo:lo + A_SUB] * jnp.exp2(bh[lo:lo + A_SUB] - bi)).astype(BF16)
            kp = (kh[:hi] * jnp.exp2(bi - bh[:hi])).astype(BF16)
            part = lax.dot_general(qi, kp, NT, preferred_element_type=F32)
            if hi < A_CH:
                part = jnp.concatenate([part, jnp.zeros((A_SUB, A_CH - hi), F32)], axis=1)
            parts.append(part)
        amat = jnp.concatenate(parts, axis=0)
        if bounded:
            amat = jnp.where(si <= ti, amat, 0.0)
        o = o + _bdot(amat.astype(BF16), vhb)
        if not bounded:
            q3 = qh.reshape(nsub, A_SUB, HD)
            k3 = kh.reshape(nsub, A_SUB, HD)
            b3 = bh.reshape(nsub, A_SUB, HD)
            v3 = vh.reshape(nsub, A_SUB, HD)
            q_lo, q_hi = q3[:, :half], q3[:, half:]
            b_lo, b_hi = b3[:, :half], b3[:, half:]
            acc_lo = jnp.zeros((nsub, half, HD), F32)
            acc_hi = jnp.zeros((nsub, half, HD), F32)
            for s in range(A_SUB):
                bs, ks, vs = b3[:, s:s + 1, :], k3[:, s:s + 1, :], v3[:, s:s + 1, :]
                if s < half:
                    e_lo = jnp.exp2(jnp.where(tsub >= s, b_lo - bs, NEG))
                    acc_lo = acc_lo + jnp.sum(q_lo * ks * e_lo, axis=-1, keepdims=True) * vs
                    e_hi = jnp.exp2(b_hi - bs)
                else:
                    e_hi = jnp.exp2(jnp.where(tsub >= s - half, b_hi - bs, NEG))
                acc_hi = acc_hi + jnp.sum(q_hi * ks * e_hi, axis=-1, keepdims=True) * vs
            o = o + jnp.concatenate([acc_lo, acc_hi], axis=1).reshape(A_CH, HD)
        st_ref[h] = st * sdec[:, sl] + _bdot(vt_ref[sl, :], kdec[:, sl].astype(BF16))
        outs.append(o)
    return outs


def _mlstm_gate_rows(grow_ref, gbc, triu_ref):
    nc = grow_ref.shape[0]
    g = grow_ref[...] + gbc
    srow = lax.broadcasted_iota(jnp.int32, g.shape, 1)
    g = jnp.where(srow < HEADS, g, _log_sigmoid(g)) * LOG2E
    cum = jnp.dot(g.reshape(nc * 8, LANES), triu_ref[...], precision=HI,
                  preferred_element_type=F32).reshape(nc, 8, LANES)
    b = cum[:, HEADS:, :]
    ab = jnp.concatenate([g[:, :HEADS, :] - b, b], axis=1)
    cm = ab.reshape(nc * 8, LANES)
    lane = lax.broadcasted_iota(jnp.int32, cm.shape, 1)
    d = 1
    while d < LANES:
        cm = jnp.maximum(cm, jnp.where(lane >= d, pltpu.roll(cm, d, 1), NEG))
        d *= 2
    return ab, cm.reshape(nc, 8, LANES)


def _mlstm_vectors(ab_ref, cm_ref, m_s):
    ab = ab_ref[...]
    a4, b4 = ab[:HEADS], ab[HEADS:]
    m_prev = m_s[0:HEADS, :]
    mx4 = jnp.maximum(cm_ref[0:HEADS, :], m_prev)
    mt4 = b4 + mx4
    mx_last = jnp.broadcast_to(mx4[:, C_CH - 1:C_CH], (HEADS, LANES))
    wg4 = jnp.exp2(a4 - mx_last)
    dec4 = jnp.exp2(m_prev - mx_last)
    m_s[0:HEADS, :] = jnp.broadcast_to(mt4[:, C_CH - 1:C_CH], (HEADS, LANES))
    cols = [(_colmat(mx4[h:h + 1]), _colmat(mt4[h:h + 1])) for h in range(HEADS)]
    return a4, wg4, dec4, m_prev, cols


def _mlstm_chunk(rows, per_batch, vecs):
    ri = lax.broadcasted_iota(jnp.int32, (C_CH, C_CH), 0)
    ci = lax.broadcasted_iota(jnp.int32, (C_CH, C_CH), 1)
    causal = ri >= ci
    ones_blk = jnp.ones((C_CH, HD), BF16)
    hcs = []
    for (cv_ref, co_ref, cq_ref, kt_ref, ct_s), (a4, wg4, dec4, m_prev, cols) in zip(per_batch, vecs):
        for h in range(HEADS):
            sl = slice(h * HD, (h + 1) * HD)
            row = slice(h, h + 1)
            mxm, mtm = cols[h]
            ct = ct_s[h]
            qb = cq_ref[rows, sl]
            qf = qb.astype(F32)
            ktb = kt_ref[sl, :]
            kt = ktb.astype(F32)
            vaug = jnp.concatenate([cv_ref[rows, sl], ones_blk], axis=1)

            w = jnp.exp2(jnp.where(causal, a4[row] - mxm, NEG))
            sc = _bdot(qb, ktb)
            lhs = jnp.concatenate([(sc * w).astype(BF16),
                                   (qf * jnp.exp2(m_prev[row] - mxm)).astype(BF16)], axis=1)
            tot = _bdot(lhs, jnp.concatenate([vaug, ct.astype(BF16)], axis=0))
            hout = tot[:, :HD] / jnp.maximum(jnp.abs(tot[:, HD:]), jnp.exp2(-mtm))

            dec2 = jnp.concatenate([dec4[row], dec4[row]], axis=1)
            ct_s[h] = dec2 * ct + _bdot((kt * wg4[row]).astype(BF16), vaug)
            hcs.append(jax.nn.sigmoid(co_ref[rows, sl].astype(F32)) * hout)
    return hcs


def _rec_body(aq_ref, af_ref, ai_ref, ag_ref, cq_ref, cv_ref, co_ref, cg_ref, kt_ref, vt_ref, grow_ref,
              lb_ref, ang_ref, gbc_ref, cng_ref, triu_ref, mean_ref,
              ya_ref, yc_ref, st_s, ab_s, cm_s, ct_s, m_s, lf_s, ka_s):
    bsz = aq_ref.shape[0]
    nchunk = R_BLK // C_CH

    @pl.when(pl.program_id(0) == 0)
    def _():
        st_s[...] = jnp.zeros_like(st_s)
        ct_s[...] = jnp.zeros_like(ct_s)
        m_s[...] = jnp.zeros_like(m_s)

    gbc = gbc_ref[...]
    for bi in range(bsz):
        ab_s[bi], cm_s[bi] = _mlstm_gate_rows(grow_ref.at[bi], gbc, triu_ref)

    lb = lb_ref[...]
    oml = 1.0 - lb
    ang = ang_ref[...]
    cng = cng_ref[...]

    def chunk(c, carry):
        rows = pl.ds(pl.multiple_of(c * C_CH, C_CH), C_CH)
        spans = [_hgrn_gates(rows, af_ref.at[bi], lb, oml, lf_s.at[bi], ka_s.at[bi]) for bi in range(bsz)]
        bounded = functools.reduce(jnp.maximum, spans) <= A_MAX_SUB_LOG2_DECAY

        def rest(is_bounded):
            vecs = [_mlstm_vectors(ab_s.at[bi, c], cm_s.at[bi, c], m_s.at[bi]) for bi in range(bsz)]
            pres = [_hgrn_prep(rows, aq_ref.at[bi], ai_ref.at[bi], ag_ref.at[bi], lf_s.at[bi], ka_s.at[bi])
                    for bi in range(bsz)]
            outs = [o for bi in range(bsz)
                    for o in _hgrn_finish(pres[bi], st_s.at[bi], vt_ref.at[bi, c], is_bounded)]
            outs = _unit_rms_heads(outs, mean_ref)
            per_batch = [(cv_ref.at[bi], co_ref.at[bi], cq_ref.at[bi], kt_ref.at[bi, c], ct_s.at[bi])
                         for bi in range(bsz)]
            hcs = _unit_rms_heads(_mlstm_chunk(rows, per_batch, vecs), mean_ref)
            for i in range(bsz * HEADS):
                bi, h = divmod(i, HEADS)
                sl = slice(h * HD, (h + 1) * HD)
                ya_ref[bi, rows, sl] = (outs[i] * ang * pres[bi][3][:, sl]).astype(ya_ref.dtype)
                yc_ref[bi, rows, sl] = (hcs[i] * cng * _silu(cg_ref[bi, rows, sl].astype(F32))
                                        ).astype(yc_ref.dtype)

        pl.when(bounded)(functools.partial(rest, True))
        pl.when(jnp.logical_not(bounded))(functools.partial(rest, False))
        return carry

    lax.fori_loop(0, nchunk, chunk, 0)


def _recurrent(proj, ckt, avt, grow, lb, ang, gbc, cng, bsz, s):
    assert A_CH == C_CH
    proj3 = proj.reshape(bsz, s, N_OUT)
    nchunk = R_BLK // C_CH
    col = lambda cid: pl.BlockSpec((bsz, R_BLK, GROUP), lambda t, cid=cid: (0, t, cid))
    full = lambda shp: pl.BlockSpec(shp, lambda t: (0,) * len(shp))
    chunk_t = pl.BlockSpec((bsz, nchunk, GROUP, C_CH), lambda t: (0, t, 0, 0))
    triu = jnp.triu(jnp.ones((C_CH, C_CH), F32))
    mean_mat = jnp.full((HD, HD), 1.0 / HD, BF16)
    out = pl.BlockSpec((bsz, R_BLK, GROUP), lambda t: (0, t, 0))
    ya, yc = pl.pallas_call(
        _rec_body,
        grid=(s // R_BLK,),
        in_specs=[col(C_AQ), col(C_AF), col(C_AI), col(C_AG),
                  col(C_CQ), col(C_CV), col(C_CO), col(C_CG), chunk_t, chunk_t,
                  pl.BlockSpec((bsz, nchunk, 8, LANES), lambda t: (0, t, 0, 0)),
                  full((1, GROUP)), full((1, HD)),
                  full((8, 1)), full((1, HD)),
                  full((C_CH, C_CH)), full((HD, HD))],
        out_specs=[out, out],
        out_shape=[jax.ShapeDtypeStruct((bsz, s, GROUP), BF16)] * 2,
        scratch_shapes=[pltpu.VMEM((bsz, HEADS, HD, HD), F32),
                        pltpu.VMEM((bsz, nchunk, 8, LANES), F32), pltpu.VMEM((bsz, nchunk, 8, LANES), F32),
                        pltpu.VMEM((bsz, HEADS, HD, 2 * HD), F32), pltpu.VMEM((bsz, 8, LANES), F32),
                        pltpu.VMEM((bsz, A_CH, GROUP), F32), pltpu.VMEM((bsz, A_CH, GROUP), F32)],
        compiler_params=_cp(("arbitrary",), 48),
        name="hgrn2_mlstm",
    )(*([proj3] * 8), ckt, avt, grow.reshape(bsz, s // LANES, 8, LANES),
      lb, ang, gbc, cng, triu, mean_mat)
    return ya.reshape(bsz * s, GROUP), yc.reshape(bsz * s, GROUP)


def _flash_body(qt_ref, k_ref, vt_ref, bg_ref, lam_ref, sg_ref, y_ref,
                qz_s, m_s, l_s, acc_s, *, lam_init, online_max):
    qi = pl.program_id(2)
    qt = qt_ref[0]
    rowi = lax.broadcasted_iota(jnp.int32, (HD, BQ), 0)
    zero = jnp.zeros_like(qt)
    qz_s[0] = jnp.where(rowi < B_DQK, qt, zero)
    qz_s[1] = jnp.where(rowi >= B_DQK, qt, zero)
    m_s[...] = jnp.full_like(m_s, NEG)
    l_s[...] = jnp.zeros_like(l_s)
    acc_s[...] = jnp.zeros_like(acc_s)

    def step(kblk, vblk, q0, mask):
        qs = slice(q0, BQ)
        scores = [_bdot(kblk, qz_s[c, :, qs]) for c in range(2)]
        for c in range(2):
            s = scores[c]
            if mask is not None:
                s = jnp.where(mask, s, NEG)
            if online_max:
                m_old = m_s[c, :, qs]
                m_new = jnp.maximum(m_old, jnp.max(s, axis=0, keepdims=True))
                alpha = jnp.exp2(m_old - m_new)
                p = jnp.exp2(s - m_new)
                l_s[c, :, qs] = alpha * l_s[c, :, qs] + jnp.sum(p, axis=0, keepdims=True)
                acc_s[c, :, qs] = alpha * acc_s[c, :, qs] + _bdot(vblk, p.astype(BF16))
                m_s[c, :, qs] = m_new
            else:
                p = jnp.exp2(s)
                l_s[c, :, qs] = l_s[c, :, qs] + jnp.sum(p, axis=0, keepdims=True)
                acc_s[c, :, qs] = acc_s[c, :, qs] + _bdot(vblk, p.astype(BF16))

    def body(kb, carry):
        r0 = pl.multiple_of(kb * BK, BK)
        step(k_ref[pl.ds(r0, BK), :], vt_ref[0, kb], 0, None)
        return carry

    kpq = BQ // BK
    lax.fori_loop(0, qi * kpq, body, 0)
    for d in range(BQ // BKD):
        q0 = d * BKD
        kidx = lax.broadcasted_iota(jnp.int32, (BKD, BQ - q0), 0)
        qidx = lax.broadcasted_iota(jnp.int32, (BKD, BQ - q0), 1)
        rows = pl.ds(pl.multiple_of(qi * BQ + q0, BKD), BKD)
        lo = q0 % BK
        vblk = vt_ref[0, qi * kpq + q0 // BK][:, lo:lo + BKD]
        step(k_ref[rows, :], vblk, q0, kidx <= qidx)

    lp = lam_ref[...]
    lam = (jnp.exp(jnp.sum(lp[0:1] * lp[1:2], axis=1, keepdims=True))
           - jnp.exp(jnp.sum(lp[2:3] * lp[3:4], axis=1, keepdims=True)) + lam_init)
    ot = acc_s[0] / l_s[0] - lam * (acc_s[1] / l_s[1])
    o = ot.T
    y = _rms(o, sg_ref[...]) * (1.0 - lam_init) * _silu(bg_ref[...].astype(F32))
    y_ref[...] = y.astype(y_ref.dtype)


def _flash_call(qt, kk, vt, proj, lam_p, sg, *, bsz, s, lam_init, online_max):
    nq = s // BQ
    return pl.pallas_call(
        functools.partial(_flash_body, lam_init=lam_init, online_max=online_max),
        grid=(bsz, HEADS, nq),
        in_specs=[pl.BlockSpec((1, HD, BQ), lambda b, h, q: (b, h, q)),
                  pl.BlockSpec((s, HD), lambda b, h, q: (b, h)),
                  pl.BlockSpec((1, s // BK, HD, BK), lambda b, h, q: (b, 0, h, 0)),
                  pl.BlockSpec((BQ, HD), lambda b, h, q: (b * nq + q, C_BG * HEADS + h)),
                  pl.BlockSpec((4, B_DQK), lambda b, h, q: (0, 0)),
                  pl.BlockSpec((1, HD), lambda b, h, q: (0, 0))],
        out_specs=pl.BlockSpec((BQ, HD), lambda b, h, q: (b * nq + q, h)),
        out_shape=jax.ShapeDtypeStruct((bsz * s, GROUP), BF16),
        scratch_shapes=[pltpu.VMEM((2, HD, BQ), BF16), pltpu.VMEM((2, 1, BQ), F32),
                        pltpu.VMEM((2, 1, BQ), F32), pltpu.VMEM((2, HD, BQ), F32)],
        compiler_params=_cp(("arbitrary", "arbitrary", "arbitrary"), 48),
        name="diff_flash_online" if online_max else "diff_flash",
    )(qt, kk, vt, proj, lam_p, sg)


def _flash(qt, kk, vt, proj, lam_p, sg, qk_g, bsz, s, lam_init):
    bound = (B_DQK ** 0.5) * LOG2E * jnp.max(jnp.abs(qk_g[0])) * jnp.max(jnp.abs(qk_g[1]))
    args = (qt, kk, vt, proj, lam_p, sg)
    call = functools.partial(_flash_call, bsz=bsz, s=s, lam_init=lam_init)
    return lax.cond(bound <= MAX_UNSHIFTED_LOG2_SCORE - math.log2(s),
                    functools.partial(call, online_max=False),
                    functools.partial(call, online_max=True), *args)


def _memkv_body(mem_ref, g_ref, w_ref, kg_ref, km_ref, vm_ref):
    mn = _rms(mem_ref[0], g_ref[...]).astype(BF16)
    kv = _bdot(mn, w_ref[...])
    for h in range(HEADS):
        sl = slice(h * HD, (h + 1) * HD)
        km_ref[0, :, sl] = _rms(kv[:, sl], kg_ref[...]).astype(BF16)
    vm_ref[0] = kv[:, GROUP:].astype(BF16)


def _memkv(mem, g, w, kg):
    bsz = mem.shape[0]
    return pl.pallas_call(
        _memkv_body,
        grid=(bsz,),
        in_specs=[pl.BlockSpec((1, N_MEM, D_MODEL), lambda b: (b, 0, 0)),
                  pl.BlockSpec((1, D_MODEL), lambda b: (0, 0)),
                  pl.BlockSpec((D_MODEL, 2 * GROUP), lambda b: (0, 0)),
                  pl.BlockSpec((1, HD), lambda b: (0, 0))],
        out_specs=[pl.BlockSpec((1, N_MEM, GROUP), lambda b: (b, 0, 0))] * 2,
        out_shape=[jax.ShapeDtypeStruct((bsz, N_MEM, GROUP), BF16)] * 2,
        compiler_params=_cp(("arbitrary",), 32),
        name="mem_kv",
    )(mem, g, w, kg)


def _merge_body(x_ref, ya_ref, yb_ref, yc_ref, xq_ref, xg_ref, km_ref, vm_ref, qg_ref, w_ref, o_ref):
    acc = x_ref[...]
    acc = acc + _bdot(ya_ref[...], w_ref[0:GROUP, :])
    acc = acc + _bdot(yb_ref[...], w_ref[GROUP:2 * GROUP, :])
    acc = acc + _bdot(yc_ref[...], w_ref[2 * GROUP:3 * GROUP, :])
    qg = qg_ref[...]
    yx = []
    for h in range(HEADS):
        sl = slice(h * HD, (h + 1) * HD)
        qn = (_rms(xq_ref[:, sl].astype(F32), qg) * (HD ** -0.5)).astype(BF16)
        s = lax.dot_general(qn, km_ref[0, :, sl], NT, preferred_element_type=F32)
        p = jnp.exp(s - jnp.max(s, axis=-1, keepdims=True))
        l = jnp.sum(p, axis=-1, keepdims=True)
        ox = _bdot(p.astype(BF16), vm_ref[0, :, sl]) / l
        yx.append((ox * _silu(xg_ref[:, sl].astype(F32))).astype(BF16))
    o_ref[...] = acc + _bdot(jnp.concatenate(yx, axis=1), w_ref[3 * GROUP:4 * GROUP, :])


def _merge(x2, ya, yb, yc, proj, km, vm, qg, w_out, bsz, s):
    nb = s // M_BM
    t = bsz * s
    blk = lambda: pl.BlockSpec((M_BM, GROUP), lambda i: (i, 0))
    col = lambda cid: pl.BlockSpec((M_BM, GROUP), lambda i, cid=cid: (i, cid))
    return pl.pallas_call(
        _merge_body,
        grid=(t // M_BM,),
        in_specs=[pl.BlockSpec((M_BM, D_MODEL), lambda i: (i, 0)),
                  blk(), blk(), blk(), col(C_XQ), col(C_XG),
                  pl.BlockSpec((1, N_MEM, GROUP), lambda i: (i // nb, 0, 0)),
                  pl.BlockSpec((1, N_MEM, GROUP), lambda i: (i // nb, 0, 0)),
                  pl.BlockSpec((1, HD), lambda i: (0, 0)),
                  pl.BlockSpec((4 * GROUP, D_MODEL), lambda i: (0, 0))],
        out_specs=pl.BlockSpec((M_BM, D_MODEL), lambda i: (i, 0)),
        out_shape=jax.ShapeDtypeStruct((t, D_MODEL), F32),
        compiler_params=_cp(("arbitrary",), 48),
        name="xattn_out_proj",
    )(x2, ya, yb, yc, proj, proj, km, vm, qg, w_out)


def _layer(x2, mem, cosm, sinm, layer_idx, lb, norm_g, w_in, mlstm_gate_b, hgrn_norm_g,
           diff_qk_norm_g, diff_lambda, diff_subln_g, mlstm_conv_w, mlstm_conv_b,
           mlstm_norm_g, mem_norm_g, w_mem_kv, xattn_qk_norm_g, w_out, bsz, s):
    g12 = 12 * GROUP
    w_main = jnp.concatenate([w_in[:, :g12], w_in[:, g12 + 2 * HEADS:]], axis=1).astype(BF16)
    wgt = jnp.pad(w_in[:, g12:g12 + 2 * HEADS].T, ((0, 8), (0, 0))).astype(BF16)
    gq = jnp.tile(diff_qk_norm_g[0], GROUP // B_DQK)[None, :]
    gk = jnp.tile(diff_qk_norm_g[1], GROUP // B_DQK)[None, :]
    proj, grow, qt, kk, vt, ckt, avt = _inproj(x2, norm_g[None, :], w_main, wgt, cosm, sinm, gq, gk,
                                               mlstm_conv_w, mlstm_conv_b[None, :], bsz, s)

    ya, yc = _recurrent(proj, ckt, avt, grow, lb[None, :], hgrn_norm_g[None, :],
                        mlstm_gate_b[:, None], mlstm_norm_g[None, :], bsz, s)

    lam_init = 0.8 - 0.6 * math.exp(-0.3 * layer_idx)
    yb = _flash(qt, kk, vt, proj, diff_lambda, diff_subln_g[None, :], diff_qk_norm_g, bsz, s, lam_init)

    km, vm = _memkv(mem, mem_norm_g[None, :], w_mem_kv.astype(BF16), xattn_qk_norm_g[1][None, :])
    return _merge(x2, ya, yb, yc, proj, km, vm, xattn_qk_norm_g[0][None, :],
                  w_out.astype(BF16), bsz, s)


def kernel(x, mem, positions, norm_g, w_in, mlstm_gate_b, hgrn_lb_logits, hgrn_norm_g,
           diff_qk_norm_g, diff_lambda, diff_subln_g, mlstm_conv_w, mlstm_conv_b,
           mlstm_norm_g, mem_norm_g, w_mem_kv, xattn_qk_norm_g, w_out):
    bsz, s, d = x.shape
    depth = norm_g.shape[0]
    assert d == D_MODEL and mem.shape[1] == N_MEM
    assert all(s % blk == 0 for blk in (IN_BM, R_BLK, P_BM, BQ, BK, M_BM)) and (bsz * s) % T_BM == 0
    cosm, sinm = _rope_tables(positions)
    sm = jax.nn.softmax(hgrn_lb_logits.astype(F32), axis=0)
    lower_bounds = jnp.cumsum(sm, axis=0) - sm[0]
    x2 = x.reshape(bsz * s, d)
    for l in range(depth):
        x2 = _layer(x2, mem, cosm, sinm, l, lower_bounds[l], norm_g[l], w_in[l], mlstm_gate_b[l],
                    hgrn_norm_g[l], diff_qk_norm_g[l], diff_lambda[l], diff_subln_g[l],
                    mlstm_conv_w[l], mlstm_conv_b[l], mlstm_norm_g[l], mem_norm_g[l],
                    w_mem_kv[l], xattn_qk_norm_g[l], w_out[l], bsz, s)
    return x2.reshape(bsz, s, d)
```
